```python
import math
import jax, jax.numpy as jnp
from jax import lax
import numpy as np

D_MODEL = 1024
BATCH = 32
SEQ = 256
DEPTH = 4
DEC_BATCH = 8
DEC_SEQ = 2048
PAST_LEN = 512

GRID_W = 64
ATT_HEADS = 4
QK_HEAD_DIM = 64
V_HEAD_DIM = 2 * QK_HEAD_DIM
ATT_WIDTH = ATT_HEADS * V_HEAD_DIM
QK_COLS = ATT_HEADS * 2 * QK_HEAD_DIM
ROPE_BASE = 10000.0
Q_BLOCK = 128
CONV_WIDTH = D_MODEL // 4
CONV_KERNEL = 31
POOL_WIDTH = D_MODEL // 4
POOL_WINDOWS = (2, 4, 8, 16)
POOL_GROUPS = 4
POOL_GROUP_DIM = POOL_WIDTH // POOL_GROUPS
N_BRANCH = 3
IN_COLS = 2 * QK_COLS + ATT_WIDTH + 2 * CONV_WIDTH + POOL_WIDTH + N_BRANCH * D_MODEL
D_FF = ((8 * D_MODEL // 3 + 127) // 128) * 128
N_MOD = 9
ALPHA = (2 * DEPTH) ** 0.25
BETA = (8 * DEPTH) ** -0.25
LN_EPS = 1e-5

kernel_name = "hybrid_diff_conv_pool_prefix_dit_step"


def layer_norm(x, g, b):
    xf = x.astype(jnp.float32)
    mu = jnp.mean(xf, axis=-1, keepdims=True)
    var = jnp.mean(jnp.square(xf - mu), axis=-1, keepdims=True)
    y = (xf - mu) * lax.rsqrt(var + LN_EPS)
    return (y * g + b).astype(x.dtype)


def rms_norm(x, g):
    xf = x.astype(jnp.float32)
    y = xf * lax.rsqrt(jnp.mean(jnp.square(xf), axis=-1, keepdims=True) + LN_EPS)
    return (y * g).astype(x.dtype)


def swiglu(h, w_in, w_down):
    gu = h @ w_in
    g, u = jnp.split(gu, 2, axis=-1)
    return (jax.nn.silu(g) * u) @ w_down


def _rotate(seg, pos):
    nf = seg.shape[-1] // 2
    inv = ROPE_BASE ** (-jnp.arange(nf, dtype=jnp.float32) / nf)
    ang = pos.astype(jnp.float32)[:, None] * inv[None, :]
    c = jnp.cos(ang)[:, None, None, :]
    s = jnp.sin(ang)[:, None, None, :]
    x1, x2 = seg[..., :nf], seg[..., nf:]
    return jnp.concatenate([x1 * c - x2 * s, x2 * c + x1 * s], axis=-1).astype(seg.dtype)


def axial_rope(x):
    L = x.shape[1]
    rows = L // GRID_W
    row = jnp.repeat(jnp.arange(rows), GRID_W)
    col = jnp.tile(jnp.arange(GRID_W), rows)
    half = x.shape[-1] // 2
    return jnp.concatenate([_rotate(x[..., :half], row), _rotate(x[..., half:], col)], axis=-1)


def diff_attention(q, k, v, lam):
    B, Lq = q.shape[0], q.shape[1]
    qb_len = min(Q_BLOCK, Lq)
    nb = Lq // qb_len
    qb = q.reshape(B, nb, qb_len, ATT_HEADS, 2, QK_HEAD_DIM).swapaxes(0, 1)
    scale = QK_HEAD_DIM ** -0.5

    def block(qi):
        s = jnp.einsum('bqhmd,bkhmd->bhmqk', qi, k).astype(jnp.float32) * scale
        p = jax.nn.softmax(s, axis=-1)
        w = p[:, :, 0] - lam * p[:, :, 1]
        return jnp.einsum('bhqk,bkhe->bqhe', w.astype(v.dtype), v)

    o = lax.map(block, qb)
    return o.swapaxes(0, 1).reshape(B, Lq, ATT_HEADS, V_HEAD_DIM)


def depthwise_conv(u, w, b):
    C = u.shape[-1]
    y = lax.conv_general_dilated(u, w[:, None, :].astype(u.dtype), window_strides=(1,),
                                 padding=[(CONV_KERNEL // 2, CONV_KERNEL // 2)],
                                 dimension_numbers=('NWC', 'WIO', 'NWC'),
                                 feature_group_count=C)
    return y + b


def multiscale_pool(u, w_grp, scale):
    B, L, _ = u.shape
    ug = u.reshape(B, L, POOL_GROUPS, POOL_GROUP_DIM).astype(jnp.float32)
    t = jnp.arange(L)
    outs = []
    for g, win in enumerate(POOL_WINDOWS):
        ui = ug[:, :, g]
        cs = jnp.concatenate([jnp.zeros_like(ui[:, :1]), jnp.cumsum(ui, axis=1)], axis=1)
        lo = jnp.clip(t - win // 2, 0, L)
        hi = jnp.clip(t + win // 2, 0, L)
        mean = (cs[:, hi] - cs[:, lo]) / (hi - lo).astype(jnp.float32)[None, :, None]
        outs.append(jnp.einsum('blc,cd->bld', (mean - ui).astype(u.dtype), w_grp[g]))
    return jnp.concatenate(outs, axis=-1) * scale


def token_mixer(h, l, p, ctx_k, ctx_v):
    B, L, _ = h.shape
    z = h @ p["w_in"][l] + p["b_in"][l]
    pts = (QK_COLS, 2 * QK_COLS, 2 * QK_COLS + ATT_WIDTH,
           2 * QK_COLS + ATT_WIDTH + 2 * CONV_WIDTH,
           2 * QK_COLS + ATT_WIDTH + 2 * CONV_WIDTH + POOL_WIDTH)
    q, k, v, cg, pu, gt = jnp.split(z, pts, axis=-1)
    q = q.reshape(B, L, ATT_HEADS, 2, QK_HEAD_DIM)
    k = k.reshape(B, L, ATT_HEADS, 2, QK_HEAD_DIM)
    v = v.reshape(B, L, ATT_HEADS, V_HEAD_DIM)
    if ctx_k is None:
        kk, vv = k, v
    else:
        q = axial_rope(q)
        kk = jnp.concatenate([ctx_k, axial_rope(k)], axis=1)
        vv = jnp.concatenate([ctx_v, v], axis=1)
    lam_init = 0.8 - 0.6 * math.exp(-0.3 * l)
    lq = p["lambda_qk"][l].astype(jnp.float32)
    lam = jnp.exp(jnp.sum(lq[0] * lq[1])) - jnp.exp(jnp.sum(lq[2] * lq[3])) + lam_init
    o = diff_attention(q, kk, vv, lam)
    o = rms_norm(o, p["subln_g"][l]) * (1.0 - lam_init)
    att_out = o.reshape(B, L, ATT_WIDTH) @ p["w_att_o"][l]
    a, g = jnp.split(cg, 2, axis=-1)
    u = a * jax.nn.sigmoid(g)
    u = depthwise_conv(u, p["conv_dw_w"][l], p["conv_dw_b"][l])
    u = jax.nn.silu(layer_norm(u, p["conv_ln_g"][l], p["conv_ln_b"][l]))
    conv_out = u @ p["w_conv_o"][l]
    pool_out = multiscale_pool(pu, p["w_pool_g"][l], p["pool_scale"][l]) @ p["w_pool_o"][l]
    ga, gc, gp = jnp.split(jax.nn.sigmoid(gt), N_BRANCH, axis=-1)
    merged = ga * att_out + gc * conv_out + gp * pool_out
    return merged @ p["w_out"][l] + p["b_out"][l], k, v


def trunk_layer(x, cond, l, p, ctx_k, ctx_v):
    mod = jax.nn.silu(cond) @ p["w_mod"][l] + p["b_mod"][l]
    mod = mod.reshape(cond.shape[0], N_MOD, D_MODEL)[:, :, None, :]

    def modulate(i):
        return x * (1.0 + mod[:, 3 * i + 1]) + mod[:, 3 * i]

    h = modulate(0)
    x = layer_norm(ALPHA * x + 0.5 * mod[:, 2] * swiglu(h, p["w_ffn_in"][l, 0], p["w_ffn_out"][l, 0]),
                   p["ln_g"][l, 0], p["ln_b"][l, 0])
    h = modulate(1)
    mix, k, v = token_mixer(h, l, p, ctx_k, ctx_v)
    x = layer_norm(ALPHA * x + mod[:, 5] * mix, p["ln_g"][l, 1], p["ln_b"][l, 1])
    h = modulate(2)
    x = layer_norm(ALPHA * x + 0.5 * mod[:, 8] * swiglu(h, p["w_ffn_in"][l, 1], p["w_ffn_out"][l, 1]),
                   p["ln_g"][l, 2], p["ln_b"][l, 2])
    return x, k, v


def setup_inputs(seed: int = 0) -> dict:
    key = jax.random.key(seed)
    ks = iter(jax.random.split(key, 40))

    def nrm(shape, scale):
        return jax.random.normal(next(ks), shape, jnp.float32) * scale

    d = {}
    d["x_prompt"] = nrm((BATCH, SEQ, D_MODEL), 1.0)
    d["x_sample"] = nrm((DEC_BATCH, DEC_SEQ, D_MODEL), 1.0)
    d["cache_k"] = nrm((DEC_BATCH, DEPTH, PAST_LEN, ATT_HEADS, 2, QK_HEAD_DIM), 1.0)
    d["cache_v"] = nrm((DEC_BATCH, DEPTH, PAST_LEN, ATT_HEADS, V_HEAD_DIM), 1.0)
    d["c"] = nrm((DEC_BATCH, D_MODEL), 1.0)
    d["c_ctx"] = nrm((D_MODEL,), 1.0)
    d["w_mod"] = nrm((DEPTH, D_MODEL, N_MOD * D_MODEL), 0.5 * D_MODEL ** -0.5)
    d["b_mod"] = nrm((DEPTH, N_MOD * D_MODEL), 0.02)
    d["w_ffn_in"] = nrm((DEPTH, 2, D_MODEL, 2 * D_FF), D_MODEL ** -0.5)
    d["w_ffn_out"] = nrm((DEPTH, 2, D_FF, D_MODEL), BETA * D_FF ** -0.5)
    d["ln_g"] = 1.0 + nrm((DEPTH, 3, D_MODEL), 0.02)
    d["ln_b"] = nrm((DEPTH, 3, D_MODEL), 0.02)
    d["w_in"] = nrm((DEPTH, D_MODEL, IN_COLS), D_MODEL ** -0.5)
    d["b_in"] = nrm((DEPTH, IN_COLS), 0.02)
    d["lambda_qk"] = nrm((DEPTH, 4, QK_HEAD_DIM), 0.1)
    d["subln_g"] = 1.0 + nrm((DEPTH, ATT_HEADS, V_HEAD_DIM), 0.02)
    d["w_att_o"] = nrm((DEPTH, ATT_WIDTH, D_MODEL), ATT_WIDTH ** -0.5)
    d["conv_dw_w"] = nrm((DEPTH, CONV_KERNEL, CONV_WIDTH), CONV_KERNEL ** -0.5)
    d["conv_dw_b"] = nrm((DEPTH, CONV_WIDTH), 0.02)
    d["conv_ln_g"] = 1.0 + nrm((DEPTH, CONV_WIDTH), 0.02)
    d["conv_ln_b"] = nrm((DEPTH, CONV_WIDTH), 0.02)
    d["w_conv_o"] = nrm((DEPTH, CONV_WIDTH, D_MODEL), CONV_WIDTH ** -0.5)
    d["w_pool_g"] = nrm((DEPTH, POOL_GROUPS, POOL_GROUP_DIM, POOL_GROUP_DIM), POOL_GROUP_DIM ** -0.5)
    d["pool_scale"] = 1.0 + nrm((DEPTH, POOL_WIDTH), 0.02)
    d["w_pool_o"] = nrm((DEPTH, POOL_WIDTH, D_MODEL), POOL_WIDTH ** -0.5)
    d["w_out"] = nrm((DEPTH, D_MODEL, D_MODEL), BETA * D_MODEL ** -0.5)
    d["b_out"] = nrm((DEPTH, D_MODEL), 0.02)
    return d


def reference(x_prompt, x_sample, cache_k, cache_v, c, c_ctx, w_mod, b_mod, w_ffn_in, w_ffn_out,
              ln_g, ln_b, w_in, b_in, lambda_qk, subln_g, w_att_o, conv_dw_w, conv_dw_b,
              conv_ln_g, conv_ln_b, w_conv_o, w_pool_g, pool_scale, w_pool_o, w_out, b_out):
    p = dict(w_mod=w_mod, b_mod=b_mod, w_ffn_in=w_ffn_in, w_ffn_out=w_ffn_out, ln_g=ln_g, ln_b=ln_b,
             w_in=w_in, b_in=b_in, lambda_qk=lambda_qk, subln_g=subln_g, w_att_o=w_att_o,
             conv_dw_w=conv_dw_w, conv_dw_b=conv_dw_b, conv_ln_g=conv_ln_g, conv_ln_b=conv_ln_b,
             w_conv_o=w_conv_o, w_pool_g=w_pool_g, pool_scale=pool_scale, w_pool_o=w_pool_o,
             w_out=w_out, b_out=b_out)
    xp = x_prompt
    cond_ctx = c_ctx[None, :]
    ks_new, vs_new = [], []
    for l in range(DEPTH):
        xp, k_l, v_l = trunk_layer(xp, cond_ctx, l, p, None, None)
        ks_new.append(k_l)
        vs_new.append(v_l)
    new_cache_k = jnp.stack(ks_new, axis=1)
    new_cache_v = jnp.stack(vs_new, axis=1)
    xs = x_sample
    for l in range(DEPTH):
        xs, _, _ = trunk_layer(xs, c, l, p, cache_k[:, l], cache_v[:, l])
    return (xp, xs, new_cache_k, new_cache_v)
```

```python
import functools
import math

import numpy as np
import jax
import jax.numpy as jnp
from jax import lax
from jax.experimental import pallas as pl
from jax.experimental.pallas import tpu as pltpu

F32 = jnp.float32
BF16 = jnp.bfloat16

D_MODEL = 1024
BATCH = 32
SEQ = 256
DEPTH = 4
DEC_BATCH = 8
DEC_SEQ = 2048
PAST_LEN = 512
GRID_W = 64
ATT_HEADS = 4
QK_HEAD_DIM = 64
V_HEAD_DIM = 2 * QK_HEAD_DIM
ATT_WIDTH = ATT_HEADS * V_HEAD_DIM
QK_COLS = ATT_HEADS * 2 * QK_HEAD_DIM
ROPE_BASE = 10000.0
CONV_WIDTH = D_MODEL // 4
CONV_KERNEL = 31
POOL_WIDTH = D_MODEL // 4
POOL_WINDOWS = (2, 4, 8, 16)
POOL_GROUPS = 4
POOL_GROUP_DIM = POOL_WIDTH // POOL_GROUPS
N_BRANCH = 3
D_FF = ((8 * D_MODEL // 3 + 127) // 128) * 128
N_MOD = 9
ALPHA = (2 * DEPTH) ** 0.25
LN_EPS = 1e-5

PROJ_COLS = 2 * QK_COLS + ATT_WIDTH + 2 * CONV_WIDTH + POOL_WIDTH
GATE_COLS = N_BRANCH * D_MODEL

N_CTX_TOK = BATCH * SEQ
N_DEC_TOK = DEC_BATCH * DEC_SEQ
N_TOK = N_CTX_TOK + N_DEC_TOK

TM = 512
N_CTX_TILES = N_CTX_TOK // TM
DEC_TILES_PER_SEQ = DEC_SEQ // TM
N_TILES = N_TOK // TM
COND_ROWS = 16
FF_CHUNK = 256
MOD_TN = 1536
TQ = 256
PAD = 16
LOCAL_CHUNK = 128

VMEM_LIMIT = 56 * 1024 * 1024


def _cond_row(i):
    return jnp.where(i < N_CTX_TILES, 0, 1 + (i - N_CTX_TILES) // DEC_TILES_PER_SEQ)


def _layer_norm(r, g, b):
    mu = jnp.mean(r, axis=-1, keepdims=True)
    d = r - mu
    var = jnp.mean(d * d, axis=-1, keepdims=True)
    return d * lax.rsqrt(var + LN_EPS) * g + b


def _dot(a, b):
    return jnp.dot(a, b, preferred_element_type=F32)


def _const_spec(shape):
    nd = len(shape)
    return pl.BlockSpec(shape, lambda *_: (0,) * nd, pipeline_mode=pl.Buffered(1))


def _mod_kernel(cond_ref, w_ref, b_ref, o_ref):
    a = jax.nn.silu(cond_ref[...]).astype(BF16)
    o_ref[0] = _dot(a, w_ref[0].astype(BF16)) + b_ref[0]


def _modulation(cond, w_mod, b_mod):
    n_col = N_MOD * D_MODEL
    return pl.pallas_call(
        _mod_kernel,
        grid=(DEPTH, n_col // MOD_TN),
        in_specs=[
            pl.BlockSpec((COND_ROWS, D_MODEL), lambda l, j: (0, 0)),
            pl.BlockSpec((1, D_MODEL, MOD_TN), lambda l, j: (l, 0, j)),
            pl.BlockSpec((1, 1, MOD_TN), lambda l, j: (l, 0, j)),
        ],
        out_specs=pl.BlockSpec((1, COND_ROWS, MOD_TN), lambda l, j: (l, 0, j)),
        out_shape=jax.ShapeDtypeStruct((DEPTH, COND_ROWS, n_col), F32),
        compiler_params=pltpu.CompilerParams(
            dimension_semantics=("arbitrary", "arbitrary"), vmem_limit_bytes=VMEM_LIMIT),
        name="modulation",
    )(cond, w_mod, b_mod.reshape(DEPTH, 1, n_col))


def _ffn_kernel(x_ref, mod_ref, win_ref, wout_ref, g_ref, b_ref, o_ref, act_ref, *, sub):
    x = x_ref[...]
    shift = mod_ref[0, 3 * sub:3 * sub + 1, :]
    scale = mod_ref[0, 3 * sub + 1:3 * sub + 2, :]
    gate = mod_ref[0, 3 * sub + 2:3 * sub + 3, :]
    h = (x * (1.0 + scale) + shift).astype(BF16)
    for j in range(D_FF // FF_CHUNK):
        lo = j * FF_CHUNK
        g = _dot(h, win_ref[:, lo:lo + FF_CHUNK])
        u = _dot(h, win_ref[:, D_FF + lo:D_FF + lo + FF_CHUNK])
        act_ref[:, lo:lo + FF_CHUNK] = (jax.nn.silu(g) * u).astype(BF16)
    y = _dot(act_ref[...], wout_ref[...])
    o_ref[...] = _layer_norm(ALPHA * x + 0.5 * gate * y, g_ref[...], b_ref[...])


def _ffn(x, mod_l, w_in, w_out, ln_g, ln_b, sub):
    return pl.pallas_call(
        functools.partial(_ffn_kernel, sub=sub),
        grid=(N_TILES,),
        in_specs=[
            pl.BlockSpec((TM, D_MODEL), lambda i: (i, 0)),
            pl.BlockSpec((1, N_MOD, D_MODEL), lambda i: (_cond_row(i), 0, 0)),
            _const_spec((D_MODEL, 2 * D_FF)),
            _const_spec((D_FF, D_MODEL)),
            _const_spec((1, D_MODEL)),
            _const_spec((1, D_MODEL)),
        ],
        out_specs=pl.BlockSpec((TM, D_MODEL), lambda i: (i, 0)),
        out_shape=jax.ShapeDtypeStruct((N_TOK, D_MODEL), F32),
        scratch_shapes=[pltpu.VMEM((TM, D_FF), BF16)],
        compiler_params=pltpu.CompilerParams(
            dimension_semantics=("arbitrary",), vmem_limit_bytes=VMEM_LIMIT),
        name="ffn",
    )(x, mod_l, w_in, w_out, ln_g.reshape(1, D_MODEL), ln_b.reshape(1, D_MODEL))


def _rope_tables():
    nf = QK_HEAD_DIM // 4
    t = np.arange(DEC_SEQ)
    lane = np.arange(V_HEAD_DIM)
    d = lane % QK_HEAD_DIM
    by_col = d >= QK_HEAD_DIM // 2
    e = d % (QK_HEAD_DIM // 2)
    inv = ROPE_BASE ** (-(e % nf).astype(np.float64) / nf)
    pos = np.where(by_col[None, :], (t % GRID_W)[:, None], (t // GRID_W)[:, None])
    ang = pos.astype(np.float64) * inv[None, :]
    first = (e < nf)[None, :]
    cos = np.cos(ang)
    sin_next = np.where(first, -np.sin(ang), 0.0)
    sin_prev = np.where(first, 0.0, np.sin(ang))
    ident = np.zeros((TM, V_HEAD_DIM))
    cos = np.concatenate([cos, ident + 1.0], axis=0)
    sin_next = np.concatenate([sin_next, ident], axis=0)
    sin_prev = np.concatenate([sin_prev, ident], axis=0)
    return (jnp.asarray(cos, F32), jnp.asarray(sin_next, F32), jnp.asarray(sin_prev, F32))


def _inproj_kernel(x_ref, mod_ref, w_ref, b_ref, cos_ref, sn_ref, sp_ref,
                   q_ref, k_ref, v_ref, u_ref, pu_ref, kf_ref, vf_ref):
    i = pl.program_id(0)
    x = x_ref[...]
    h = (x * (1.0 + mod_ref[0, 4:5, :]) + mod_ref[0, 3:4, :]).astype(BF16)
    cos = cos_ref[...]
    sn = sn_ref[...]
    sp = sp_ref[...]
    half = QK_HEAD_DIM // 4

    def rope(z):
        up = pltpu.roll(z, V_HEAD_DIM - half, axis=1)
        dn = pltpu.roll(z, half, axis=1)
        return z * cos + up * sn + dn * sp

    is_ctx = i < N_CTX_TILES
    for hd in range(ATT_HEADS):
        lo = hd * V_HEAD_DIM
        zq = _dot(h, w_ref[:, lo:lo + V_HEAD_DIM]) + b_ref[:, lo:lo + V_HEAD_DIM]
        q_ref[:, lo:lo + V_HEAD_DIM] = (rope(zq) * (QK_HEAD_DIM ** -0.5)).astype(BF16)
        zk = _dot(h, w_ref[:, QK_COLS + lo:QK_COLS + lo + V_HEAD_DIM]) \
            + b_ref[:, QK_COLS + lo:QK_COLS + lo + V_HEAD_DIM]
        k_ref[:, lo:lo + V_HEAD_DIM] = rope(zk).astype(BF16)

        @pl.when(is_ctx)
        def _():
            kf_ref[:, lo:lo + V_HEAD_DIM] = zk

    c0 = 2 * QK_COLS
    zv = _dot(h, w_ref[:, c0:c0 + ATT_WIDTH]) + b_ref[:, c0:c0 + ATT_WIDTH]
    v_ref[...] = zv.astype(BF16)

    @pl.when(is_ctx)
    def _():
        vf_ref[...] = zv

    c1 = c0 + ATT_WIDTH
    za = _dot(h, w_ref[:, c1:c1 + CONV_WIDTH]) + b_ref[:, c1:c1 + CONV_WIDTH]
    zg = _dot(h, w_ref[:, c1 + CONV_WIDTH:c1 + 2 * CONV_WIDTH]) \
        + b_ref[:, c1 + CONV_WIDTH:c1 + 2 * CONV_WIDTH]
    u_ref[...] = za * jax.nn.sigmoid(zg)
    c2 = c1 + 2 * CONV_WIDTH
    pu_ref[...] = _dot(h, w_ref[:, c2:c2 + POOL_WIDTH]) + b_ref[:, c2:c2 + POOL_WIDTH]


def _inproj(x, mod_l, w_proj, b_proj, tables):
    cos, sn, sp = tables

    def tab_map(i):
        return (jnp.where(i < N_CTX_TILES, DEC_TILES_PER_SEQ,
                          (i - N_CTX_TILES) % DEC_TILES_PER_SEQ), 0)

    def ctx_map(i):
        return (jnp.minimum(i, N_CTX_TILES - 1), 0)

    tab_spec = pl.BlockSpec((TM, V_HEAD_DIM), tab_map)
    row = lambda w: pl.BlockSpec((TM, w), lambda i: (i, 0))
    return pl.pallas_call(
        _inproj_kernel,
        grid=(N_TILES,),
        in_specs=[
            row(D_MODEL),
            pl.BlockSpec((1, N_MOD, D_MODEL), lambda i: (_cond_row(i), 0, 0)),
            _const_spec((D_MODEL, PROJ_COLS)),
            _const_spec((1, PROJ_COLS)),
            tab_spec, tab_spec, tab_spec,
        ],
        out_specs=[
            row(QK_COLS), row(QK_COLS), row(ATT_WIDTH), row(CONV_WIDTH), row(POOL_WIDTH),
            pl.BlockSpec((TM, QK_COLS), ctx_map),
            pl.BlockSpec((TM, ATT_WIDTH), ctx_map),
        ],
        out_shape=[
            jax.ShapeDtypeStruct((N_TOK, QK_COLS), BF16),
            jax.ShapeDtypeStruct((N_TOK, QK_COLS), BF16),
            jax.ShapeDtypeStruct((N_TOK, ATT_WIDTH), BF16),
            jax.ShapeDtypeStruct((N_TOK, CONV_WIDTH), F32),
            jax.ShapeDtypeStruct((N_TOK, POOL_WIDTH), F32),
            jax.ShapeDtypeStruct((N_CTX_TOK, QK_COLS), F32),
            jax.ShapeDtypeStruct((N_CTX_TOK, ATT_WIDTH), F32),
        ],
        compiler_params=pltpu.CompilerParams(
            dimension_semantics=("arbitrary",), vmem_limit_bytes=VMEM_LIMIT),
        name="inproj",
    )(x, mod_l, w_proj, b_proj, cos, sn, sp)


def _attn_kernel(*refs, n_pieces, lam_init):
    lq_ref, g_ref, q_ref = refs[:3]
    kv_refs = refs[3:3 + 2 * n_pieces]
    o_ref = refs[3 + 2 * n_pieces]

    lq = lq_ref[...]
    lam = (jnp.exp(jnp.sum(lq[0:1] * lq[1:2], axis=-1, keepdims=True))
           - jnp.exp(jnp.sum(lq[2:3] * lq[3:4], axis=-1, keepdims=True)) + lam_init)

    q = q_ref[...]
    lane = lax.broadcasted_iota(jnp.int32, q.shape, 1)
    zero = jnp.zeros_like(q)
    q_maps = (jnp.where(lane < QK_HEAD_DIM, q, zero), jnp.where(lane >= QK_HEAD_DIM, q, zero))
    nt = (((1,), (1,)), ((), ()))

    ks = [kv_refs[2 * p][...].astype(BF16) for p in range(n_pieces)]
    probs = []
    for qm in q_maps:
        s = [lax.dot_general(qm, k, nt, preferred_element_type=F32) for k in ks]
        m = functools.reduce(jnp.maximum, [jnp.max(sp, axis=-1, keepdims=True) for sp in s])
        e = [jnp.exp(sp - m) for sp in s]
        den = functools.reduce(lambda a, b: a + b, [jnp.sum(ep, axis=-1, keepdims=True) for ep in e])
        inv = 1.0 / den
        probs.append([ep * inv for ep in e])
    o = None
    for p in range(n_pieces):
        w = (probs[0][p] - lam * probs[1][p]).astype(BF16)
        t = _dot(w, kv_refs[2 * p + 1][...].astype(BF16))
        o = t if o is None else o + t
    y = o * lax.rsqrt(jnp.mean(o * o, axis=-1, keepdims=True) + LN_EPS)
    o_ref[...] = ((y * g_ref[0]) * (1.0 - lam_init)).astype(o_ref.dtype)


def _attention(q, k, v, lam_qk, subln_g, lam_init, *, n_batch, seq, row0, ctx_kv=None):
    nq = seq // TQ
    q0 = row0 // TQ
    s0 = row0 // seq
    head = lambda b, h, j: (0, 0)
    in_specs = [
        pl.BlockSpec((4, QK_HEAD_DIM), head),
        pl.BlockSpec((1, 1, V_HEAD_DIM), lambda b, h, j: (h, 0, 0)),
        pl.BlockSpec((TQ, V_HEAD_DIM), lambda b, h, j: (q0 + b * nq + j, h)),
    ]
    args = [lam_qk, subln_g.reshape(ATT_HEADS, 1, V_HEAD_DIM), q]
    n_pieces = 1
    if ctx_kv is not None:
        n_pieces = 2
        ck, cv = ctx_kv
        past = pl.BlockSpec((None, PAST_LEN, V_HEAD_DIM), lambda b, h, j: (b, 0, h))
        in_specs += [past, past]
        args += [ck, cv]
    own = pl.BlockSpec((seq, V_HEAD_DIM), lambda b, h, j: (s0 + b, h))
    in_specs += [own, own]
    args += [k, v]
    return pl.pallas_call(
        functools.partial(_attn_kernel, n_pieces=n_pieces, lam_init=lam_init),
        grid=(n_batch, ATT_HEADS, nq),
        in_specs=in_specs,
        out_specs=pl.BlockSpec((TQ, V_HEAD_DIM), lambda b, h, j: (b * nq + j, h)),
        out_shape=jax.ShapeDtypeStruct((n_batch * seq, ATT_WIDTH), BF16),
        compiler_params=pltpu.CompilerParams(
            dimension_semantics=("arbitrary", "arbitrary", "arbitrary"),
            vmem_limit_bytes=VMEM_LIMIT),
        name="attention",
    )(*args)


def _local_kernel(u_ref, pu_ref, cw_ref, cb_ref, g_ref, b_ref, cu_ref, pd_ref,
                  upad, ppad, s2, s4, s8, *, seq):
    zpad = jnp.zeros((PAD, CONV_WIDTH), F32)
    for buf in (upad, ppad, s2, s4, s8):
        buf[0:PAD, :] = zpad
        buf[PAD + seq:PAD + seq + PAD, :] = zpad
    upad[PAD:PAD + seq, :] = u_ref[...]
    ppad[PAD:PAD + seq, :] = pu_ref[...]

    n_chunks = seq // LOCAL_CHUNK
    r = LOCAL_CHUNK
    centre = CONV_KERNEL // 2

    for c in range(n_chunks):
        base = PAD + c * r
        acc = jnp.zeros((r, CONV_WIDTH), F32)
        for t in range(CONV_KERNEL):
            acc = acc + upad[base + t - centre:base + t - centre + r, :] * cw_ref[t:t + 1, :]
        y = _layer_norm(acc + cb_ref[...], g_ref[...], b_ref[...])
        cu_ref[c * r:(c + 1) * r, :] = jax.nn.silu(y).astype(cu_ref.dtype)

    ext = seq + PAD
    s2[8:8 + ext, :] = ppad[7:7 + ext, :] + ppad[8:8 + ext, :]
    s4[8:8 + ext, :] = s2[7:7 + ext, :] + s2[9:9 + ext, :]
    s8[8:8 + ext, :] = s4[6:6 + ext, :] + s4[10:10 + ext, :]
    lane = lax.broadcasted_iota(jnp.int32, (r, POOL_WIDTH), 1)
    grp = lane // POOL_GROUP_DIM
    half_win = jnp.where(grp == 0, 1, jnp.where(grp == 1, 2, jnp.where(grp == 2, 4, 8)))
    for c in range(n_chunks):
        base = PAD + c * r
        sl = slice(base, base + r)
        s16 = s8[base - 4:base - 4 + r, :] + s8[base + 4:base + 4 + r, :]
        tot = jnp.where(grp == 0, s2[sl, :],
                        jnp.where(grp == 1, s4[sl, :], jnp.where(grp == 2, s8[sl, :], s16)))
        t = lax.broadcasted_iota(jnp.int32, (r, POOL_WIDTH), 0) + c * r
        cnt = jnp.clip(t + half_win, 0, seq) - jnp.clip(t - half_win, 0, seq)
        pd_ref[c * r:(c + 1) * r, :] = (tot / cnt.astype(F32) - ppad[sl, :]).astype(pd_ref.dtype)


def _local(u, pu, conv_w, conv_b, ln_g, ln_b, *, n_seq, seq, row0):
    s0 = row0 // seq
    blk = pl.BlockSpec((seq, CONV_WIDTH), lambda b: (s0 + b, 0))
    out = pl.BlockSpec((seq, CONV_WIDTH), lambda b: (b, 0))
    vec = pl.BlockSpec((1, CONV_WIDTH), lambda b: (0, 0))
    pad_buf = pltpu.VMEM((seq + 2 * PAD, CONV_WIDTH), F32)
    return pl.pallas_call(
        functools.partial(_local_kernel, seq=seq),
        grid=(n_seq,),
        in_specs=[blk, blk, pl.BlockSpec((CONV_KERNEL, CONV_WIDTH), lambda b: (0, 0)), vec, vec, vec],
        out_specs=[out, out],
        out_shape=[jax.ShapeDtypeStruct((n_seq * seq, CONV_WIDTH), BF16),
                   jax.ShapeDtypeStruct((n_seq * seq, POOL_WIDTH), BF16)],
        scratch_shapes=[pad_buf] * 5,
        compiler_params=pltpu.CompilerParams(
            dimension_semantics=("arbitrary",), vmem_limit_bytes=VMEM_LIMIT),
        name="local_mixers",
    )(u, pu, conv_w, conv_b.reshape(1, -1), ln_g.reshape(1, -1), ln_b.reshape(1, -1))


def _merge_kernel(x_ref, mod_ref, o_ref, cu_ref, pd_ref, wg_ref, bg_ref, wa_ref, wc_ref,
                  wgrp_ref, ps_ref, wp_ref, wo_ref, bo_ref, g_ref, b_ref, out_ref, merged_ref):
    x = x_ref[...]
    h = (x * (1.0 + mod_ref[0, 4:5, :]) + mod_ref[0, 3:4, :]).astype(BF16)
    pooled = (_dot(pd_ref[...], wgrp_ref[...]) * ps_ref[...]).astype(BF16)
    att = o_ref[...]
    cu = cu_ref[...]
    cw = 256
    for j in range(D_MODEL // cw):
        sl = slice(j * cw, (j + 1) * cw)
        branches = (_dot(att, wa_ref[:, sl]), _dot(cu, wc_ref[:, sl]), _dot(pooled, wp_ref[:, sl]))
        merged = None
        for n, br in enumerate(branches):
            gs = slice(n * D_MODEL + j * cw, n * D_MODEL + (j + 1) * cw)
            gate = jax.nn.sigmoid(_dot(h, wg_ref[:, gs]) + bg_ref[:, gs])
            merged = gate * br if merged is None else merged + gate * br
        merged_ref[:, sl] = merged.astype(BF16)
    mix = _dot(merged_ref[...], wo_ref[...]) + bo_ref[...]
    out_ref[...] = _layer_norm(ALPHA * x + mod_ref[0, 5:6, :] * mix, g_ref[...], b_ref[...])


def _merge(x, mod_l, o, cu, pd, w_gate, b_gate, w_att_o, w_conv_o, w_grp, pool_scale, w_pool_o,
           w_out, b_out, ln_g, ln_b):
    row = lambda w: pl.BlockSpec((TM, w), lambda i: (i, 0))
    return pl.pallas_call(
        _merge_kernel,
        grid=(N_TILES,),
        in_specs=[
            row(D_MODEL),
            pl.BlockSpec((1, N_MOD, D_MODEL), lambda i: (_cond_row(i), 0, 0)),
            row(ATT_WIDTH), row(CONV_WIDTH), row(POOL_WIDTH),
            _const_spec((D_MODEL, GATE_COLS)), _const_spec((1, GATE_COLS)),
            _const_spec((ATT_WIDTH, D_MODEL)), _const_spec((CONV_WIDTH, D_MODEL)),
            _const_spec((POOL_WIDTH, POOL_WIDTH)), _const_spec((1, POOL_WIDTH)),
            _const_spec((POOL_WIDTH, D_MODEL)),
            _const_spec((D_MODEL, D_MODEL)), _const_spec((1, D_MODEL)),
            _const_spec((1, D_MODEL)), _const_spec((1, D_MODEL)),
        ],
        out_specs=row(D_MODEL),
        out_shape=jax.ShapeDtypeStruct((N_TOK, D_MODEL), F32),
        scratch_shapes=[pltpu.VMEM((TM, D_MODEL), BF16)],
        compiler_params=pltpu.CompilerParams(
            dimension_semantics=("arbitrary",), vmem_limit_bytes=VMEM_LIMIT),
        name="merge",
    )(x, mod_l, o, cu, pd, w_gate, b_gate, w_att_o, w_conv_o, w_grp, pool_scale, w_pool_o,
      w_out, b_out, ln_g, ln_b)


def _block_diag(w_grp):
    eye = jnp.eye(POOL_GROUPS, dtype=w_grp.dtype)
    return jnp.einsum("gcd,gh->gchd", w_grp, eye).reshape(POOL_WIDTH, POOL_WIDTH)


def kernel(x_prompt, x_sample, cache_k, cache_v, c, c_ctx, w_mod, b_mod, w_ffn_in, w_ffn_out, ln_g, ln_b, w_in, b_in, lambda_qk, subln_g, w_att_o, conv_dw_w, conv_dw_b, conv_ln_g, conv_ln_b, w_conv_o, w_pool_g, pool_scale, w_pool_o, w_out, b_out):
    x = jnp.concatenate([x_prompt.reshape(N_CTX_TOK, D_MODEL), x_sample.reshape(N_DEC_TOK, D_MODEL)], axis=0)
    cond = jnp.zeros((COND_ROWS, D_MODEL), F32).at[0].set(c_ctx).at[1:1 + DEC_BATCH].set(c)
    mod = _modulation(cond, w_mod, b_mod).reshape(DEPTH, COND_ROWS, N_MOD, D_MODEL)
    tables = _rope_tables()
    past_k = cache_k.reshape(DEC_BATCH, DEPTH, PAST_LEN, QK_COLS)
    past_v = cache_v.reshape(DEC_BATCH, DEPTH, PAST_LEN, ATT_WIDTH)
    vec = lambda a: a.reshape(1, -1)

    new_k, new_v = [], []
    for l in range(DEPTH):
        lam_init = 0.8 - 0.6 * math.exp(-0.3 * l)
        x = _ffn(x, mod[l], w_ffn_in[l, 0].astype(BF16), w_ffn_out[l, 0].astype(BF16),
                 ln_g[l, 0], ln_b[l, 0], sub=0)
        q, k, v, u, pu, kf, vf = _inproj(x, mod[l], w_in[l, :, :PROJ_COLS].astype(BF16),
                                         vec(b_in[l, :PROJ_COLS]), tables)
        new_k.append(kf.reshape(BATCH, SEQ, ATT_HEADS, 2, QK_HEAD_DIM))
        new_v.append(vf.reshape(BATCH, SEQ, ATT_HEADS, V_HEAD_DIM))
        o_ctx = _attention(q, k, v, lambda_qk[l], subln_g[l], lam_init,
                           n_batch=BATCH, seq=SEQ, row0=0)
        o_dec = _attention(q, k, v, lambda_qk[l], subln_g[l], lam_init,
                           n_batch=DEC_BATCH, seq=DEC_SEQ, row0=N_CTX_TOK,
                           ctx_kv=(past_k[:, l], past_v[:, l]))
        loc = (conv_dw_w[l], conv_dw_b[l], conv_ln_g[l], conv_ln_b[l])
        cu_ctx, pd_ctx = _local(u, pu, *loc, n_seq=BATCH, seq=SEQ, row0=0)
        cu_dec, pd_dec = _local(u, pu, *loc, n_seq=DEC_BATCH, seq=DEC_SEQ, row0=N_CTX_TOK)
        cat = lambda a, b: jnp.concatenate([a, b], axis=0)
        x = _merge(x, mod[l], cat(o_ctx, o_dec), cat(cu_ctx, cu_dec), cat(pd_ctx, pd_dec),
                   w_in[l, :, PROJ_COLS:].astype(BF16), vec(b_in[l, PROJ_COLS:]),
                   w_att_o[l].astype(BF16), w_conv_o[l].astype(BF16),
                   _block_diag(w_pool_g[l]).astype(BF16), vec(pool_scale[l]),
                   w_pool_o[l].astype(BF16), w_out[l].astype(BF16), vec(b_out[l]),
                   vec(ln_g[l, 1]), vec(ln_b[l, 1]))
        x = _ffn(x, mod[l], w_ffn_in[l, 1].astype(BF16), w_ffn_out[l, 1].astype(BF16),
                 ln_g[l, 2], ln_b[l, 2], sub=2)

    y_prompt = x[:N_CTX_TOK].reshape(BATCH, SEQ, D_MODEL)
    y_sample = x[N_CTX_TOK:].reshape(DEC_BATCH, DEC_SEQ, D_MODEL)
    return (y_prompt, y_sample, jnp.stack(new_k, axis=1), jnp.stack(new_v, axis=1))
```

```python
import functools
import math

import numpy as np
import jax
import jax.numpy as jnp
from jax import lax
from jax.experimental import pallas as pl
from jax.experimental.pallas import tpu as pltpu

F32 = jnp.float32
BF16 = jnp.bfloat16

D_MODEL = 1024
BATCH = 32
SEQ = 256
DEPTH = 4
DEC_BATCH = 8
DEC_SEQ = 2048
PAST_LEN = 512
GRID_W = 64
ATT_HEADS = 4
QK_HEAD_DIM = 64
V_HEAD_DIM = 2 * QK_HEAD_DIM
ATT_WIDTH = ATT_HEADS * V_HEAD_DIM
QK_COLS = ATT_HEADS * 2 * QK_HEAD_DIM
ROPE_BASE = 10000.0
CONV_WIDTH = D_MODEL // 4
CONV_KERNEL = 31
POOL_WIDTH = D_MODEL // 4
POOL_WINDOWS = (2, 4, 8, 16)
POOL_GROUPS = 4
POOL_GROUP_DIM = POOL_WIDTH // POOL_GROUPS
N_BRANCH = 3
D_FF = ((8 * D_MODEL // 3 + 127) // 128) * 128
N_MOD = 9
ALPHA = (2 * DEPTH) ** 0.25
LN_EPS = 1e-5

PROJ_COLS = 2 * QK_COLS + ATT_WIDTH + 2 * CONV_WIDTH + POOL_WIDTH
GATE_COLS = N_BRANCH * D_MODEL

N_CTX_TOK = BATCH * SEQ
N_DEC_TOK = DEC_BATCH * DEC_SEQ
N_TOK = N_CTX_TOK + N_DEC_TOK

SUBLANES = 8
LANES = 128
TM = 512
N_CTX_TILES = N_CTX_TOK // TM
DEC_TILES_PER_SEQ = DEC_SEQ // TM
N_TILES = N_TOK // TM
COND_ROWS = 16
FF_CHUNK = 256
MOD_TN = 1536
ATT_ROWS = 256
KEY_CHUNK = 256
ROW_BLOCK = 64
DEC_TQ = 4 * ATT_ROWS
CTX_SEQS_PER_STEP = 2
PAD = 16
LOCAL_CHUNK = 128
CONV_SHIFT_GROUPS = -(-(CONV_KERNEL + 1) // SUBLANES)
CONV_HALO = (CONV_SHIFT_GROUPS - 1) * SUBLANES

Q_SCALE = QK_HEAD_DIM ** -0.5 * math.log2(math.e)

VMEM_LIMIT = 56 * 1024 * 1024


def _cond_row(i):
    return jnp.where(i < N_CTX_TILES, 0, 1 + (i - N_CTX_TILES) // DEC_TILES_PER_SEQ)


def _ctx_tile(i):
    return (jnp.minimum(i, N_CTX_TILES - 1), 0)


def _dec_tile(i):
    return (jnp.maximum(i - N_CTX_TILES, 0), 0)


def _group_specs(width):
    return [pl.BlockSpec((TM, width), _ctx_tile), pl.BlockSpec((TM, width), _dec_tile)]


def _layer_norm(r, g, b):
    mu = jnp.mean(r, axis=-1, keepdims=True)
    d = r - mu
    var = jnp.mean(d * d, axis=-1, keepdims=True)
    return d * lax.rsqrt(var + LN_EPS) * g + b


def _dot(a, b):
    return jnp.dot(a, b, preferred_element_type=F32)


def _const_spec(shape):
    nd = len(shape)
    return pl.BlockSpec(shape, lambda *_: (0,) * nd, pipeline_mode=pl.Buffered(1))


def _params(n_grid_dims):
    return pltpu.CompilerParams(dimension_semantics=("arbitrary",) * n_grid_dims,
                                vmem_limit_bytes=VMEM_LIMIT)


def _mod_kernel(cond_ref, w_ref, b_ref, o_ref):
    a = jax.nn.silu(cond_ref[...]).astype(BF16)
    o_ref[0] = _dot(a, w_ref[0].astype(BF16)) + b_ref[0]


def _modulation(cond, w_mod, b_mod):
    n_col = N_MOD * D_MODEL
    return pl.pallas_call(
        _mod_kernel,
        grid=(DEPTH, n_col // MOD_TN),
        in_specs=[
            pl.BlockSpec((COND_ROWS, D_MODEL), lambda l, j: (0, 0)),
            pl.BlockSpec((1, D_MODEL, MOD_TN), lambda l, j: (l, 0, j)),
            pl.BlockSpec((1, 1, MOD_TN), lambda l, j: (l, 0, j)),
        ],
        out_specs=pl.BlockSpec((1, COND_ROWS, MOD_TN), lambda l, j: (l, 0, j)),
        out_shape=jax.ShapeDtypeStruct((DEPTH, COND_ROWS, n_col), F32),
        compiler_params=_params(2),
        name="modulation",
    )(cond, w_mod, b_mod.reshape(DEPTH, 1, n_col))


def _ffn_kernel(*refs, sub, split_in, split_out):
    n_x = 2 if split_in else 1
    n_o = 2 if split_out else 1
    x_refs = refs[:n_x]
    mod_ref, win_ref, wout_ref, g_ref, b_ref = refs[n_x:n_x + 5]
    o_refs = refs[n_x + 5:n_x + 5 + n_o]
    act_ref = refs[n_x + 5 + n_o]
    is_ctx = pl.program_id(0) < N_CTX_TILES
    x = jnp.where(is_ctx, x_refs[0][...], x_refs[1][...]) if split_in else x_refs[0][...]
    shift = mod_ref[0, 3 * sub:3 * sub + 1, :]
    scale = mod_ref[0, 3 * sub + 1:3 * sub + 2, :]
    gate = mod_ref[0, 3 * sub + 2:3 * sub + 3, :]
    h = (x * (1.0 + scale) + shift).astype(BF16)
    for j in range(D_FF // FF_CHUNK):
        lo = j * FF_CHUNK
        g = _dot(h, win_ref[:, lo:lo + FF_CHUNK])
        u = _dot(h, win_ref[:, D_FF + lo:D_FF + lo + FF_CHUNK])
        act_ref[:, lo:lo + FF_CHUNK] = (jax.nn.silu(g) * u).astype(BF16)
    y = _dot(act_ref[...], wout_ref[...])
    out = _layer_norm(ALPHA * x + 0.5 * gate * y, g_ref[...], b_ref[...])
    if split_out:
        @pl.when(is_ctx)
        def _():
            o_refs[0][...] = out

        @pl.when(jnp.logical_not(is_ctx))
        def _():
            o_refs[1][...] = out
    else:
        o_refs[0][...] = out


def _ffn(x, mod_l, w_in, w_out, ln_g, ln_b, sub, split_out=False):
    split_in = isinstance(x, tuple)
    xs = x if split_in else (x,)
    row = pl.BlockSpec((TM, D_MODEL), lambda i: (i, 0))
    if split_out:
        out_specs = _group_specs(D_MODEL)
        out_shape = [jax.ShapeDtypeStruct((N_CTX_TOK, D_MODEL), F32),
                     jax.ShapeDtypeStruct((N_DEC_TOK, D_MODEL), F32)]
    else:
        out_specs = row
        out_shape = jax.ShapeDtypeStruct((N_TOK, D_MODEL), F32)
    return pl.pallas_call(
        functools.partial(_ffn_kernel, sub=sub, split_in=split_in, split_out=split_out),
        grid=(N_TILES,),
        in_specs=(_group_specs(D_MODEL) if split_in else [row]) + [
            pl.BlockSpec((1, N_MOD, D_MODEL), lambda i: (_cond_row(i), 0, 0)),
            _const_spec((D_MODEL, 2 * D_FF)),
            _const_spec((D_FF, D_MODEL)),
            _const_spec((1, D_MODEL)),
            _const_spec((1, D_MODEL)),
        ],
        out_specs=out_specs,
        out_shape=out_shape,
        scratch_shapes=[pltpu.VMEM((TM, D_FF), BF16)],
        compiler_params=_params(1),
        name="ffn",
    )(*xs, mod_l, w_in, w_out, ln_g.reshape(1, D_MODEL), ln_b.reshape(1, D_MODEL))


def _rope_tables():
    nf = QK_HEAD_DIM // 4
    t = np.arange(DEC_SEQ)
    lane = np.arange(V_HEAD_DIM)
    d = lane % QK_HEAD_DIM
    by_col = d >= QK_HEAD_DIM // 2
    e = d % (QK_HEAD_DIM // 2)
    inv = ROPE_BASE ** (-(e % nf).astype(np.float64) / nf)
    pos = np.where(by_col[None, :], (t % GRID_W)[:, None], (t // GRID_W)[:, None])
    ang = pos.astype(np.float64) * inv[None, :]
    first = (e < nf)[None, :]
    cos = np.cos(ang)
    sin_next = np.where(first, -np.sin(ang), 0.0)
    sin_prev = np.where(first, 0.0, np.sin(ang))
    ident = np.zeros((TM, V_HEAD_DIM))
    cos = np.concatenate([cos, ident + 1.0], axis=0)
    sin_next = np.concatenate([sin_next, ident], axis=0)
    sin_prev = np.concatenate([sin_prev, ident], axis=0)
    return (jnp.asarray(cos, F32), jnp.asarray(sin_next, F32), jnp.asarray(sin_prev, F32))


def _inproj_kernel(x_ref, mod_ref, w_ref, b_ref, cos_ref, sn_ref, sp_ref,
                   q_ref, k_ref, v_ref, u_ref, pu_ref, kf_ref, vf_ref):
    is_ctx = pl.program_id(0) < N_CTX_TILES
    x = x_ref[...]
    h = (x * (1.0 + mod_ref[0, 4:5, :]) + mod_ref[0, 3:4, :]).astype(BF16)
    cos = cos_ref[...]
    sn = sn_ref[...]
    sp = sp_ref[...]
    half = QK_HEAD_DIM // 4

    def proj(lo, width):
        return _dot(h, w_ref[:, lo:lo + width]) + b_ref[:, lo:lo + width]

    def rope(z):
        up = pltpu.roll(z, V_HEAD_DIM - half, axis=1)
        dn = pltpu.roll(z, half, axis=1)
        return z * cos + up * sn + dn * sp

    zq = proj(0, QK_COLS)
    zk = proj(QK_COLS, QK_COLS)
    for hd in range(ATT_HEADS):
        sl = slice(hd * V_HEAD_DIM, (hd + 1) * V_HEAD_DIM)
        q_ref[:, sl] = (rope(zq[:, sl]) * Q_SCALE).astype(BF16)
        k_ref[:, sl] = rope(zk[:, sl]).astype(BF16)
    zv = proj(2 * QK_COLS, ATT_WIDTH)
    v_ref[...] = zv.astype(BF16)

    @pl.when(is_ctx)
    def _():
        kf_ref[...] = zk
        vf_ref[...] = zv

    c1 = 2 * QK_COLS + ATT_WIDTH
    zc = proj(c1, 2 * CONV_WIDTH)
    u_ref[...] = zc[:, :CONV_WIDTH] * jax.nn.sigmoid(zc[:, CONV_WIDTH:])
    pu_ref[...] = proj(c1 + 2 * CONV_WIDTH, POOL_WIDTH)


def _inproj(x, mod_l, w_proj, b_proj, tables):
    cos, sn, sp = tables

    def tab_map(i):
        return (jnp.where(i < N_CTX_TILES, DEC_TILES_PER_SEQ,
                          (i - N_CTX_TILES) % DEC_TILES_PER_SEQ), 0)

    tab_spec = pl.BlockSpec((TM, V_HEAD_DIM), tab_map)
    row = lambda w: pl.BlockSpec((TM, w), lambda i: (i, 0))
    return pl.pallas_call(
        _inproj_kernel,
        grid=(N_TILES,),
        in_specs=[
            row(D_MODEL),
            pl.BlockSpec((1, N_MOD, D_MODEL), lambda i: (_cond_row(i), 0, 0)),
            _const_spec((D_MODEL, PROJ_COLS)),
            _const_spec((1, PROJ_COLS)),
            tab_spec, tab_spec, tab_spec,
        ],
        out_specs=[
            row(QK_COLS), row(QK_COLS), row(ATT_WIDTH), row(CONV_WIDTH), row(POOL_WIDTH),
            pl.BlockSpec((TM, QK_COLS), _ctx_tile),
            pl.BlockSpec((TM, ATT_WIDTH), _ctx_tile),
        ],
        out_shape=[
            jax.ShapeDtypeStruct((N_TOK, QK_COLS), BF16),
            jax.ShapeDtypeStruct((N_TOK, QK_COLS), BF16),
            jax.ShapeDtypeStruct((N_TOK, ATT_WIDTH), BF16),
            jax.ShapeDtypeStruct((N_TOK, CONV_WIDTH), F32),
            jax.ShapeDtypeStruct((N_TOK, POOL_WIDTH), F32),
            jax.ShapeDtypeStruct((N_CTX_TOK, QK_COLS), F32),
            jax.ShapeDtypeStruct((N_CTX_TOK, ATT_WIDTH), F32),
        ],
        compiler_params=_params(1),
        name="inproj",
    )(x, mod_l, w_proj, b_proj, cos, sn, sp)


def _interleave(*stages):
    for i in range(max(len(st) for st in stages)):
        for st in stages:
            if i < len(st):
                st[i]()


def _attention_units(units, lam, lam_init, bufs, n_keys, group):
    n_chunks = n_keys // KEY_CHUNK
    n_tiles = n_keys // LANES
    n_blocks = 2 * ATT_ROWS // ROW_BLOCK
    nt = (((1,), (1,)), ((), ()))
    state = [dict() for _ in units]
    n_groups = len(units) // group

    def buf(n):
        return bufs[(n // group) % 2][n % group]

    def q_stage(n):
        s_ref = buf(n)[0]
        st = state[n]

        def first():
            q = units[n][0]()
            lane = lax.broadcasted_iota(jnp.int32, q.shape, 1)
            zero = jnp.zeros_like(q)
            st["qq"] = jnp.concatenate([jnp.where(lane < QK_HEAD_DIM, q, zero),
                                        jnp.where(lane >= QK_HEAD_DIM, q, zero)], axis=0)

        def chunk(c):
            cols = slice(c * KEY_CHUNK, (c + 1) * KEY_CHUNK)
            s_ref[:, cols] = lax.dot_general(st["qq"], units[n][1](c), nt,
                                             preferred_element_type=F32)
        return [first] + [functools.partial(chunk, c) for c in range(n_chunks)]

    def e_stage(n):
        s_ref, p_ref = buf(n)
        dens = state[n]["dens"] = [None] * n_blocks

        def block(r):
            rows = slice(r * ROW_BLOCK, (r + 1) * ROW_BLOCK)
            tile = lambda t: s_ref[rows, t * LANES:(t + 1) * LANES]
            m = functools.reduce(jnp.maximum, [tile(t) for t in range(n_tiles)])
            m = jnp.broadcast_to(jnp.max(m, axis=-1, keepdims=True), (ROW_BLOCK, LANES))
            d = None
            for t in range(n_tiles):
                e = jnp.exp2(tile(t) - m)
                d = e if d is None else d + e
                p_ref[rows, t * LANES:(t + 1) * LANES] = e.astype(BF16)
            dens[r] = jnp.sum(d, axis=-1, keepdims=True)
        return [functools.partial(block, r) for r in range(n_blocks)]

    def v_stage(n):
        p_ref = buf(n)[1]
        st = state[n]

        def first():
            dens = st["dens"]
            st["d1"] = jnp.concatenate(dens[:n_blocks // 2], axis=0)
            d2 = jnp.concatenate(dens[n_blocks // 2:], axis=0)
            st["c"] = (lam * st["d1"] / d2).astype(BF16)
            st["o"] = jnp.zeros((ATT_ROWS, V_HEAD_DIM), F32)

        def chunk(c):
            cols = slice(c * KEY_CHUNK, (c + 1) * KEY_CHUNK)
            w = p_ref[0:ATT_ROWS, cols] - st["c"] * p_ref[ATT_ROWS:2 * ATT_ROWS, cols]
            st["o"] = st["o"] + _dot(w, units[n][2](c))

        def last():
            o = st["o"] * (1.0 / st["d1"])
            y = o * lax.rsqrt(jnp.mean(o * o, axis=-1, keepdims=True) + LN_EPS)
            units[n][4]((y * units[n][3]()) * (1.0 - lam_init))
        return [first] + [functools.partial(chunk, c) for c in range(n_chunks)] + [last]

    for t in range(n_groups + 2):
        stages = []
        for stage, g in ((q_stage, t), (e_stage, t - 1), (v_stage, t - 2)):
            if 0 <= g < n_groups:
                stages += [stage(n) for n in range(g * group, (g + 1) * group)]
        _interleave(*stages)


def _lambda(lq_ref, lam_init):
    lq = lq_ref[...]
    return (jnp.exp(jnp.sum(lq[0:1] * lq[1:2], axis=-1, keepdims=True))
            - jnp.exp(jnp.sum(lq[2:3] * lq[3:4], axis=-1, keepdims=True)) + lam_init)


def _attn_ctx_kernel(lq_ref, g_ref, q_ref, k_ref, v_ref, o_ref, s_all, p_all, *, lam_init):
    def store(rows, cols, y):
        o_ref[rows, cols] = y.astype(o_ref.dtype)

    units = []
    for sq in range(CTX_SEQS_PER_STEP):
        rows = slice(sq * SEQ, (sq + 1) * SEQ)
        for hd in range(ATT_HEADS):
            cols = slice(hd * V_HEAD_DIM, (hd + 1) * V_HEAD_DIM)
            units.append((lambda rows=rows, cols=cols: q_ref[rows, cols],
                          lambda c, rows=rows, cols=cols: k_ref[rows, cols],
                          lambda c, rows=rows, cols=cols: v_ref[rows, cols],
                          lambda hd=hd: g_ref[hd],
                          functools.partial(store, rows, cols)))
    bufs = [[(s_all.at[par * ATT_HEADS + i], p_all.at[par * ATT_HEADS + i])
             for i in range(ATT_HEADS)] for par in range(2)]
    _attention_units(units, _lambda(lq_ref, lam_init), lam_init, bufs, SEQ, group=ATT_HEADS)


def _attention_ctx(q, k, v, lam_qk, subln_g, lam_init):
    rows = CTX_SEQS_PER_STEP * SEQ
    blk = pl.BlockSpec((rows, ATT_WIDTH), lambda b: (b, 0))
    s_buf = pltpu.VMEM((2 * ATT_HEADS, 2 * ATT_ROWS, SEQ), F32)
    p_buf = pltpu.VMEM((2 * ATT_HEADS, 2 * ATT_ROWS, SEQ), BF16)
    return pl.pallas_call(
        functools.partial(_attn_ctx_kernel, lam_init=lam_init),
        grid=(BATCH // CTX_SEQS_PER_STEP,),
        in_specs=[
            pl.BlockSpec((4, QK_HEAD_DIM), lambda b: (0, 0)),
            pl.BlockSpec((ATT_HEADS, 1, V_HEAD_DIM), lambda b: (0, 0, 0)),
            blk, blk, blk,
        ],
        out_specs=blk,
        out_shape=jax.ShapeDtypeStruct((N_CTX_TOK, ATT_WIDTH), BF16),
        scratch_shapes=[s_buf, p_buf],
        compiler_params=_params(1),
        name="attention_ctx",
    )(lam_qk, subln_g.reshape(ATT_HEADS, 1, V_HEAD_DIM), q, k, v)


def _attn_dec_kernel(lq_ref, g_ref, q_ref, pk_ref, pv_ref, k_ref, v_ref, o_ref,
                     kcat, vcat, s0, s1, p0, p1, *, lam_init):
    @pl.when(pl.program_id(2) == 0)
    def _():
        kcat[0:PAST_LEN, :] = pk_ref[...].astype(BF16)
        kcat[PAST_LEN:PAST_LEN + DEC_SEQ, :] = k_ref[...]
        vcat[0:PAST_LEN, :] = pv_ref[...].astype(BF16)
        vcat[PAST_LEN:PAST_LEN + DEC_SEQ, :] = v_ref[...]

    def store(rows, y):
        o_ref[rows, :] = y.astype(o_ref.dtype)

    def chunk_of(ref):
        return lambda c: ref[c * KEY_CHUNK:(c + 1) * KEY_CHUNK, :]

    units = []
    for r0 in range(0, DEC_TQ, ATT_ROWS):
        rows = slice(r0, r0 + ATT_ROWS)
        units.append((lambda rows=rows: q_ref[rows, :], chunk_of(kcat), chunk_of(vcat),
                      lambda: g_ref[0], functools.partial(store, rows)))
    _attention_units(units, _lambda(lq_ref, lam_init), lam_init, [[(s0, p0)], [(s1, p1)]],
                     PAST_LEN + DEC_SEQ, group=1)


def _attention_dec(q, k, v, past_k, past_v, lam_qk, subln_g, lam_init):
    nq = DEC_SEQ // DEC_TQ
    q0 = N_CTX_TOK // DEC_TQ
    s0 = N_CTX_TOK // DEC_SEQ
    n_keys = PAST_LEN + DEC_SEQ
    past = pl.BlockSpec((None, PAST_LEN, V_HEAD_DIM), lambda b, h, j: (b, 0, h))
    own = pl.BlockSpec((DEC_SEQ, V_HEAD_DIM), lambda b, h, j: (s0 + b, h))
    kv_buf = pltpu.VMEM((n_keys, V_HEAD_DIM), BF16)
    s_buf = pltpu.VMEM((2 * ATT_ROWS, n_keys), F32)
    p_buf = pltpu.VMEM((2 * ATT_ROWS, n_keys), BF16)
    return pl.pallas_call(
        functools.partial(_attn_dec_kernel, lam_init=lam_init),
        grid=(DEC_BATCH, ATT_HEADS, nq),
        in_specs=[
            pl.BlockSpec((4, QK_HEAD_DIM), lambda b, h, j: (0, 0)),
            pl.BlockSpec((1, 1, V_HEAD_DIM), lambda b, h, j: (h, 0, 0)),
            pl.BlockSpec((DEC_TQ, V_HEAD_DIM), lambda b, h, j: (q0 + b * nq + j, h)),
            past, past, own, own,
        ],
        out_specs=pl.BlockSpec((DEC_TQ, V_HEAD_DIM), lambda b, h, j: (b * nq + j, h)),
        out_shape=jax.ShapeDtypeStruct((N_DEC_TOK, ATT_WIDTH), BF16),
        scratch_shapes=[kv_buf, kv_buf, s_buf, s_buf, p_buf, p_buf],
        compiler_params=_params(3),
        name="attention_dec",
    )(lam_qk, subln_g.reshape(ATT_HEADS, 1, V_HEAD_DIM), q, past_k, past_v, k, v)


def _local_kernel(u_ref, pu_ref, cw_ref, cb_ref, g_ref, b_ref, cu_ref, pd_ref,
                  upad, ppad, s2, s4, s8, shifted, *, seq):
    zpad = jnp.zeros((PAD, CONV_WIDTH), F32)
    for buf in (upad, ppad, s2, s4, s8):
        buf[0:PAD, :] = zpad
        buf[PAD + seq:PAD + seq + PAD, :] = zpad
    upad[PAD:PAD + seq, :] = u_ref[...]
    ppad[PAD:PAD + seq, :] = pu_ref[...]

    n_chunks = seq // LOCAL_CHUNK
    r = LOCAL_CHUNK
    centre = CONV_KERNEL // 2
    n_shift_groups = CONV_SHIFT_GROUPS
    halo = CONV_HALO

    for c in range(n_chunks):
        win0 = c * r
        for shift in range(1, SUBLANES):
            shifted[shift] = upad[win0 + shift:win0 + shift + r + halo, :]
        acc = jnp.zeros((r, CONV_WIDTH), F32)
        for shift in range(SUBLANES):
            for a in range(n_shift_groups):
                t = SUBLANES * a + shift - (PAD - centre)
                if 0 <= t < CONV_KERNEL:
                    lo = SUBLANES * a
                    window = (shifted[shift, lo:lo + r, :] if shift
                              else upad[win0 + lo:win0 + lo + r, :])
                    acc = acc + window * cw_ref[t:t + 1, :]
        y = _layer_norm(acc + cb_ref[...], g_ref[...], b_ref[...])
        cu_ref[c * r:(c + 1) * r, :] = jax.nn.silu(y).astype(cu_ref.dtype)

    ext = seq + PAD
    s2[8:8 + ext, :] = ppad[7:7 + ext, :] + ppad[8:8 + ext, :]
    s4[8:8 + ext, :] = s2[7:7 + ext, :] + s2[9:9 + ext, :]
    s8[8:8 + ext, :] = s4[6:6 + ext, :] + s4[10:10 + ext, :]
    lane = lax.broadcasted_iota(jnp.int32, (r, POOL_WIDTH), 1)
    grp = lane // POOL_GROUP_DIM
    half_win = jnp.where(grp == 0, 1, jnp.where(grp == 1, 2, jnp.where(grp == 2, 4, 8)))
    for c in range(n_chunks):
        base = PAD + c * r
        sl = slice(base, base + r)
        s16 = s8[base - 4:base - 4 + r, :] + s8[base + 4:base + 4 + r, :]
        tot = jnp.where(grp == 0, s2[sl, :],
                        jnp.where(grp == 1, s4[sl, :], jnp.where(grp == 2, s8[sl, :], s16)))
        t = lax.broadcasted_iota(jnp.int32, (r, POOL_WIDTH), 0) + c * r
        cnt = jnp.clip(t + half_win, 0, seq) - jnp.clip(t - half_win, 0, seq)
        pd_ref[c * r:(c + 1) * r, :] = (tot / cnt.astype(F32) - ppad[sl, :]).astype(pd_ref.dtype)


def _local(u, pu, conv_w, conv_b, ln_g, ln_b, *, n_seq, seq, row0):
    s0 = row0 // seq
    blk = pl.BlockSpec((seq, CONV_WIDTH), lambda b: (s0 + b, 0))
    out = pl.BlockSpec((seq, CONV_WIDTH), lambda b: (b, 0))
    vec = pl.BlockSpec((1, CONV_WIDTH), lambda b: (0, 0))
    pad_buf = pltpu.VMEM((seq + 2 * PAD, CONV_WIDTH), F32)
    return pl.pallas_call(
        functools.partial(_local_kernel, seq=seq),
        grid=(n_seq,),
        in_specs=[blk, blk, pl.BlockSpec((CONV_KERNEL, CONV_WIDTH), lambda b: (0, 0)), vec, vec, vec],
        out_specs=[out, out],
        out_shape=[jax.ShapeDtypeStruct((n_seq * seq, CONV_WIDTH), BF16),
                   jax.ShapeDtypeStruct((n_seq * seq, POOL_WIDTH), BF16)],
        scratch_shapes=[pad_buf] * 5 + [
            pltpu.VMEM((SUBLANES, LOCAL_CHUNK + CONV_HALO, CONV_WIDTH), F32)],
        compiler_params=_params(1),
        name="local_mixers",
    )(u, pu, conv_w, conv_b.reshape(1, -1), ln_g.reshape(1, -1), ln_b.reshape(1, -1))


def _merge_kernel(x_ref, mod_ref, oc_ref, od_ref, cuc_ref, cud_ref, pdc_ref, pdd_ref,
                  wg_ref, bg_ref, wa_ref, wc_ref, wgrp_ref, ps_ref, wp_ref, wo_ref, bo_ref,
                  g_ref, b_ref, out_ref, merged_ref):
    is_ctx = pl.program_id(0) < N_CTX_TILES
    x = x_ref[...]
    h = (x * (1.0 + mod_ref[0, 4:5, :]) + mod_ref[0, 3:4, :]).astype(BF16)
    att = jnp.where(is_ctx, oc_ref[...], od_ref[...])
    cu = jnp.where(is_ctx, cuc_ref[...], cud_ref[...])
    pd = jnp.where(is_ctx, pdc_ref[...], pdd_ref[...])
    pooled = (_dot(pd, wgrp_ref[...]) * ps_ref[...]).astype(BF16)
    cw = 256
    for j in range(D_MODEL // cw):
        sl = slice(j * cw, (j + 1) * cw)
        branches = (_dot(att, wa_ref[:, sl]), _dot(cu, wc_ref[:, sl]), _dot(pooled, wp_ref[:, sl]))
        merged = None
        for n, br in enumerate(branches):
            gs = slice(n * D_MODEL + j * cw, n * D_MODEL + (j + 1) * cw)
            gate = jax.nn.sigmoid(_dot(h, wg_ref[:, gs]) + bg_ref[:, gs])
            merged = gate * br if merged is None else merged + gate * br
        merged_ref[:, sl] = merged.astype(BF16)
    mix = _dot(merged_ref[...], wo_ref[...]) + bo_ref[...]
    out_ref[...] = _layer_norm(ALPHA * x + mod_ref[0, 5:6, :] * mix, g_ref[...], b_ref[...])


def _merge(x, mod_l, o, cu, pd, w_gate, b_gate, w_att_o, w_conv_o, w_grp, pool_scale, w_pool_o,
           w_out, b_out, ln_g, ln_b):
    row = lambda w: pl.BlockSpec((TM, w), lambda i: (i, 0))
    return pl.pallas_call(
        _merge_kernel,
        grid=(N_TILES,),
        in_specs=[
            row(D_MODEL),
            pl.BlockSpec((1, N_MOD, D_MODEL), lambda i: (_cond_row(i), 0, 0)),
            *_group_specs(ATT_WIDTH), *_group_specs(CONV_WIDTH), *_group_specs(POOL_WIDTH),
            _const_spec((D_MODEL, GATE_COLS)), _const_spec((1, GATE_COLS)),
            _const_spec((ATT_WIDTH, D_MODEL)), _const_spec((CONV_WIDTH, D_MODEL)),
            _const_spec((POOL_WIDTH, POOL_WIDTH)), _const_spec((1, POOL_WIDTH)),
            _const_spec((POOL_WIDTH, D_MODEL)),
            _const_spec((D_MODEL, D_MODEL)), _const_spec((1, D_MODEL)),
            _const_spec((1, D_MODEL)), _const_spec((1, D_MODEL)),
        ],
        out_specs=row(D_MODEL),
        out_shape=jax.ShapeDtypeStruct((N_TOK, D_MODEL), F32),
        scratch_shapes=[pltpu.VMEM((TM, D_MODEL), BF16)],
        compiler_params=_params(1),
        name="merge",
    )(x, mod_l, *o, *cu, *pd, w_gate, b_gate, w_att_o, w_conv_o, w_grp, pool_scale, w_pool_o,
      w_out, b_out, ln_g, ln_b)


def _block_diag(w_grp):
    eye = jnp.eye(POOL_GROUPS, dtype=w_grp.dtype)
    return jnp.einsum("gcd,gh->gchd", w_grp, eye).reshape(POOL_WIDTH, POOL_WIDTH)


def kernel(x_prompt, x_sample, cache_k, cache_v, c, c_ctx, w_mod, b_mod, w_ffn_in, w_ffn_out, ln_g, ln_b, w_in, b_in, lambda_qk, subln_g, w_att_o, conv_dw_w, conv_dw_b, conv_ln_g, conv_ln_b, w_conv_o, w_pool_g, pool_scale, w_pool_o, w_out, b_out):
    x = (x_prompt.reshape(N_CTX_TOK, D_MODEL), x_sample.reshape(N_DEC_TOK, D_MODEL))
    cond = jnp.zeros((COND_ROWS, D_MODEL), F32).at[0].set(c_ctx).at[1:1 + DEC_BATCH].set(c)
    mod = _modulation(cond, w_mod, b_mod).reshape(DEPTH, COND_ROWS, N_MOD, D_MODEL)
    tables = _rope_tables()
    past_k = cache_k.reshape(DEC_BATCH, DEPTH, PAST_LEN, QK_COLS)
    past_v = cache_v.reshape(DEC_BATCH, DEPTH, PAST_LEN, ATT_WIDTH)
    vec = lambda a: a.reshape(1, -1)

    new_k, new_v = [], []
    for l in range(DEPTH):
        lam_init = 0.8 - 0.6 * math.exp(-0.3 * l)
        x = _ffn(x, mod[l], w_ffn_in[l, 0].astype(BF16), w_ffn_out[l, 0].astype(BF16),
                 ln_g[l, 0], ln_b[l, 0], sub=0)
        q, k, v, u, pu, kf, vf = _inproj(x, mod[l], w_in[l, :, :PROJ_COLS].astype(BF16),
                                         vec(b_in[l, :PROJ_COLS]), tables)
        new_k.append(kf.reshape(BATCH, SEQ, ATT_HEADS, 2, QK_HEAD_DIM))
        new_v.append(vf.reshape(BATCH, SEQ, ATT_HEADS, V_HEAD_DIM))
        o = (_attention_ctx(q, k, v, lambda_qk[l], subln_g[l], lam_init),
             _attention_dec(q, k, v, past_k[:, l], past_v[:, l], lambda_qk[l], subln_g[l], lam_init))
        loc = (conv_dw_w[l], conv_dw_b[l], conv_ln_g[l], conv_ln_b[l])
        cu_ctx, pd_ctx = _local(u, pu, *loc, n_seq=BATCH, seq=SEQ, row0=0)
        cu_dec, pd_dec = _local(u, pu, *loc, n_seq=DEC_BATCH, seq=DEC_SEQ, row0=N_CTX_TOK)
        x = _merge(x, mod[l], o, (cu_ctx, cu_dec), (pd_ctx, pd_dec),
                   w_in[l, :, PROJ_COLS:].astype(BF16), vec(b_in[l, PROJ_COLS:]),
                   w_att_o[l].astype(BF16), w_conv_o[l].astype(BF16),
                   _block_diag(w_pool_g[l]).astype(BF16), vec(pool_scale[l]),
                   w_pool_o[l].astype(BF16), w_out[l].astype(BF16), vec(b_out[l]),
                   vec(ln_g[l, 1]), vec(ln_b[l, 1]))
        x = _ffn(x, mod[l], w_ffn_in[l, 1].astype(BF16), w_ffn_out[l, 1].astype(BF16),
                 ln_g[l, 2], ln_b[l, 2], sub=2, split_out=(l == DEPTH - 1))

    y_prompt = x[0].reshape(BATCH, SEQ, D_MODEL)
    y_sample = x[1].reshape(DEC_BATCH, DEC_SEQ, D_MODEL)
    return (y_prompt, y_sample, jnp.stack(new_k, axis=1), jnp.stack(new_v, axis=1))
```

```python
import functools
import math

import numpy as np
import jax
import jax.numpy as jnp
from jax import lax
from jax.experimental import pallas as pl
from jax.experimental.pallas import tpu as pltpu

F32 = jnp.float32
BF16 = jnp.bfloat16

D_MODEL = 1024
BATCH = 32
SEQ = 256
DEPTH = 4
DEC_BATCH = 8
DEC_SEQ = 2048
PAST_LEN = 512
GRID_W = 64
ATT_HEADS = 4
QK_HEAD_DIM = 64
V_HEAD_DIM = 2 * QK_HEAD_DIM
ATT_WIDTH = ATT_HEADS * V_HEAD_DIM
QK_COLS = ATT_HEADS * 2 * QK_HEAD_DIM
ROPE_BASE = 10000.0
CONV_WIDTH = D_MODEL // 4
CONV_KERNEL = 31
POOL_WIDTH = D_MODEL // 4
POOL_WINDOWS = (2, 4, 8, 16)
POOL_GROUPS = 4
POOL_GROUP_DIM = POOL_WIDTH // POOL_GROUPS
N_BRANCH = 3
D_FF = ((8 * D_MODEL // 3 + 127) // 128) * 128
N_MOD = 9
ALPHA = (2 * DEPTH) ** 0.25
LN_EPS = 1e-5

PROJ_COLS = 2 * QK_COLS + ATT_WIDTH + 2 * CONV_WIDTH + POOL_WIDTH
GATE_COLS = N_BRANCH * D_MODEL

N_CTX_TOK = BATCH * SEQ
N_DEC_TOK = DEC_BATCH * DEC_SEQ
N_TOK = N_CTX_TOK + N_DEC_TOK

SUBLANES = 8
LANES = 128
TM = 1024
N_CTX_TILES = N_CTX_TOK // TM
DEC_TILES_PER_SEQ = DEC_SEQ // TM
N_TILES = N_TOK // TM
COND_ROWS = 16
FF_CHUNK = 256
MOD_TN = 1536
ATT_ROWS = 256
KEY_CHUNK = 256
ROW_BLOCK = 64
DEC_TQ = 8 * ATT_ROWS
CTX_SEQS_PER_STEP = 2
PAD = 16
LOCAL_CHUNK = 128
CONV_SHIFT_GROUPS = -(-(CONV_KERNEL + 1) // SUBLANES)
CONV_HALO = (CONV_SHIFT_GROUPS - 1) * SUBLANES

Q_SCALE = QK_HEAD_DIM ** -0.5 * math.log2(math.e)

VMEM_LIMIT = 56 * 1024 * 1024


def _cond_row(i):
    return jnp.where(i < N_CTX_TILES, 0, 1 + (i - N_CTX_TILES) // DEC_TILES_PER_SEQ)


def _ctx_tile(i):
    return (jnp.minimum(i, N_CTX_TILES - 1), 0)


def _dec_tile(i):
    return (jnp.maximum(i - N_CTX_TILES, 0), 0)


def _group_specs(width):
    return [pl.BlockSpec((TM, width), _ctx_tile), pl.BlockSpec((TM, width), _dec_tile)]


def _layer_norm(r, g, b):
    mu = jnp.mean(r, axis=-1, keepdims=True)
    d = r - mu
    var = jnp.mean(d * d, axis=-1, keepdims=True)
    return d * lax.rsqrt(var + LN_EPS) * g + b


def _dot(a, b):
    return jnp.dot(a, b, preferred_element_type=F32)


def _fixed_spec(tail, *lead):
    index = tuple(lead) + (0,) * len(tail)
    return pl.BlockSpec((None,) * len(lead) + tuple(tail), lambda *_: index,
                        pipeline_mode=pl.Buffered(1))


def _mod_spec(l):
    return pl.BlockSpec((None, 1, N_MOD, D_MODEL), lambda i: (l, _cond_row(i), 0, 0))


def _params(n_grid_dims):
    return pltpu.CompilerParams(dimension_semantics=("arbitrary",) * n_grid_dims,
                                vmem_limit_bytes=VMEM_LIMIT)


def _mod_kernel(cond_ref, w_ref, b_ref, o_ref):
    a = jax.nn.silu(cond_ref[...]).astype(BF16)
    o_ref[0] = _dot(a, w_ref[0].astype(BF16)) + b_ref[0]


def _modulation(cond, w_mod, b_mod):
    n_col = N_MOD * D_MODEL
    return pl.pallas_call(
        _mod_kernel,
        grid=(DEPTH, n_col // MOD_TN),
        in_specs=[
            pl.BlockSpec((COND_ROWS, D_MODEL), lambda l, j: (0, 0)),
            pl.BlockSpec((1, D_MODEL, MOD_TN), lambda l, j: (l, 0, j)),
            pl.BlockSpec((1, 1, MOD_TN), lambda l, j: (l, 0, j)),
        ],
        out_specs=pl.BlockSpec((1, COND_ROWS, MOD_TN), lambda l, j: (l, 0, j)),
        out_shape=jax.ShapeDtypeStruct((DEPTH, COND_ROWS, n_col), F32),
        compiler_params=_params(2),
        name="modulation",
    )(cond, w_mod, b_mod.reshape(DEPTH, 1, n_col))


def _ffn_kernel(*refs, sub, split_in, split_out):
    n_x = 2 if split_in else 1
    n_o = 2 if split_out else 1
    x_refs = refs[:n_x]
    mod_ref, win_ref, wout_ref, g_ref, b_ref = refs[n_x:n_x + 5]
    o_refs = refs[n_x + 5:n_x + 5 + n_o]
    act_ref = refs[n_x + 5 + n_o]
    is_ctx = pl.program_id(0) < N_CTX_TILES
    x = jnp.where(is_ctx, x_refs[0][...], x_refs[1][...]) if split_in else x_refs[0][...]
    shift = mod_ref[0, 3 * sub:3 * sub + 1, :]
    scale = mod_ref[0, 3 * sub + 1:3 * sub + 2, :]
    gate = mod_ref[0, 3 * sub + 2:3 * sub + 3, :]
    h = (x * (1.0 + scale) + shift).astype(BF16)
    for j in range(D_FF // FF_CHUNK):
        lo = j * FF_CHUNK
        g = _dot(h, win_ref[:, lo:lo + FF_CHUNK])
        u = _dot(h, win_ref[:, D_FF + lo:D_FF + lo + FF_CHUNK])
        act_ref[:, lo:lo + FF_CHUNK] = (jax.nn.silu(g) * u).astype(BF16)
    y = _dot(act_ref[...], wout_ref[...])
    out = _layer_norm(ALPHA * x + 0.5 * gate * y, g_ref[...], b_ref[...])
    if split_out:
        @pl.when(is_ctx)
        def _():
            o_refs[0][...] = out

        @pl.when(jnp.logical_not(is_ctx))
        def _():
            o_refs[1][...] = out
    else:
        o_refs[0][...] = out


def _ffn(x, p, l, sub, split_out=False):
    split_in = isinstance(x, tuple)
    xs = x if split_in else (x,)
    row = pl.BlockSpec((TM, D_MODEL), lambda i: (i, 0))
    if split_out:
        out_specs = _group_specs(D_MODEL)
        out_shape = [jax.ShapeDtypeStruct((N_CTX_TOK, D_MODEL), F32),
                     jax.ShapeDtypeStruct((N_DEC_TOK, D_MODEL), F32)]
    else:
        out_specs = row
        out_shape = jax.ShapeDtypeStruct((N_TOK, D_MODEL), F32)
    return pl.pallas_call(
        functools.partial(_ffn_kernel, sub=sub, split_in=split_in, split_out=split_out),
        grid=(N_TILES,),
        in_specs=(_group_specs(D_MODEL) if split_in else [row]) + [
            _mod_spec(l),
            _fixed_spec((D_MODEL, 2 * D_FF), l, sub // 2),
            _fixed_spec((D_FF, D_MODEL), l, sub // 2),
            _fixed_spec((1, D_MODEL), l, sub),
            _fixed_spec((1, D_MODEL), l, sub),
        ],
        out_specs=out_specs,
        out_shape=out_shape,
        scratch_shapes=[pltpu.VMEM((TM, D_FF), BF16)],
        compiler_params=_params(1),
        name="ffn",
    )(*xs, p["mod"], p["w_ffn_in"], p["w_ffn_out"], p["ln_g"], p["ln_b"])


def _rope_tables():
    nf = QK_HEAD_DIM // 4
    t = np.arange(DEC_SEQ)
    lane = np.arange(V_HEAD_DIM)
    d = lane % QK_HEAD_DIM
    by_col = d >= QK_HEAD_DIM // 2
    e = d % (QK_HEAD_DIM // 2)
    inv = ROPE_BASE ** (-(e % nf).astype(np.float64) / nf)
    pos = np.where(by_col[None, :], (t % GRID_W)[:, None], (t // GRID_W)[:, None])
    ang = pos.astype(np.float64) * inv[None, :]
    first = (e < nf)[None, :]
    cos = np.cos(ang)
    sin_next = np.where(first, -np.sin(ang), 0.0)
    sin_prev = np.where(first, 0.0, np.sin(ang))
    ident = np.zeros((TM, V_HEAD_DIM))
    cos = np.concatenate([cos, ident + 1.0], axis=0)
    sin_next = np.concatenate([sin_next, ident], axis=0)
    sin_prev = np.concatenate([sin_prev, ident], axis=0)
    return (jnp.asarray(cos, F32), jnp.asarray(sin_next, F32), jnp.asarray(sin_prev, F32))


def _inproj_kernel(x_ref, mod_ref, w_ref, b_ref, cos_ref, sn_ref, sp_ref,
                   q_ref, k_ref, v_ref, u_ref, pu_ref, kf_ref, vf_ref):
    is_ctx = pl.program_id(0) < N_CTX_TILES
    x = x_ref[...]
    h = (x * (1.0 + mod_ref[0, 4:5, :]) + mod_ref[0, 3:4, :]).astype(BF16)
    cos = cos_ref[...]
    sn = sn_ref[...]
    sp = sp_ref[...]
    half = QK_HEAD_DIM // 4

    def proj(lo, width):
        return _dot(h, w_ref[:, lo:lo + width]) + b_ref[:, lo:lo + width]

    def rope(z):
        up = pltpu.roll(z, V_HEAD_DIM - half, axis=1)
        dn = pltpu.roll(z, half, axis=1)
        return z * cos + up * sn + dn * sp

    zq = proj(0, QK_COLS)
    zk = proj(QK_COLS, QK_COLS)
    for hd in range(ATT_HEADS):
        sl = slice(hd * V_HEAD_DIM, (hd + 1) * V_HEAD_DIM)
        q_ref[:, sl] = (rope(zq[:, sl]) * Q_SCALE).astype(BF16)
        k_ref[:, sl] = rope(zk[:, sl]).astype(BF16)
    zv = proj(2 * QK_COLS, ATT_WIDTH)
    v_ref[...] = zv.astype(BF16)

    @pl.when(is_ctx)
    def _():
        kf_ref[...] = zk
        vf_ref[...] = zv

    c1 = 2 * QK_COLS + ATT_WIDTH
    zc = proj(c1, 2 * CONV_WIDTH)
    u_ref[...] = zc[:, :CONV_WIDTH] * jax.nn.sigmoid(zc[:, CONV_WIDTH:])
    pu_ref[...] = proj(c1 + 2 * CONV_WIDTH, POOL_WIDTH)


def _inproj(x, p, l):
    cos, sn, sp = p["rope"]

    def tab_map(i):
        return (jnp.where(i < N_CTX_TILES, DEC_TILES_PER_SEQ,
                          (i - N_CTX_TILES) % DEC_TILES_PER_SEQ), 0)

    tab_spec = pl.BlockSpec((TM, V_HEAD_DIM), tab_map)
    row = lambda w: pl.BlockSpec((TM, w), lambda i: (i, 0))
    return pl.pallas_call(
        _inproj_kernel,
        grid=(N_TILES,),
        in_specs=[
            row(D_MODEL),
            _mod_spec(l),
            _fixed_spec((D_MODEL, PROJ_COLS), l),
            _fixed_spec((1, PROJ_COLS), l),
            tab_spec, tab_spec, tab_spec,
        ],
        out_specs=[
            row(QK_COLS), row(QK_COLS), row(ATT_WIDTH), row(CONV_WIDTH), row(POOL_WIDTH),
            pl.BlockSpec((TM, QK_COLS), _ctx_tile),
            pl.BlockSpec((TM, ATT_WIDTH), _ctx_tile),
        ],
        out_shape=[
            jax.ShapeDtypeStruct((N_TOK, QK_COLS), BF16),
            jax.ShapeDtypeStruct((N_TOK, QK_COLS), BF16),
            jax.ShapeDtypeStruct((N_TOK, ATT_WIDTH), BF16),
            jax.ShapeDtypeStruct((N_TOK, CONV_WIDTH), F32),
            jax.ShapeDtypeStruct((N_TOK, POOL_WIDTH), F32),
            jax.ShapeDtypeStruct((N_CTX_TOK, QK_COLS), F32),
            jax.ShapeDtypeStruct((N_CTX_TOK, ATT_WIDTH), F32),
        ],
        compiler_params=_params(1),
        name="inproj",
    )(x, p["mod"], p["w_proj"], p["b_in"], cos, sn, sp)


def _interleave(*stages):
    for i in range(max(len(st) for st in stages)):
        for st in stages:
            if i < len(st):
                st[i]()


def _attention_units(units, lam, lam_init, bufs, n_keys, group):
    n_chunks = n_keys // KEY_CHUNK
    n_tiles = n_keys // LANES
    n_blocks = 2 * ATT_ROWS // ROW_BLOCK
    nt = (((1,), (1,)), ((), ()))
    state = [dict() for _ in units]
    n_groups = len(units) // group

    def buf(n):
        return bufs[(n // group) % 2][n % group]

    def q_stage(n):
        s_ref = buf(n)[0]
        st = state[n]

        def first():
            q = units[n][0]()
            lane = lax.broadcasted_iota(jnp.int32, q.shape, 1)
            zero = jnp.zeros_like(q)
            st["qq"] = jnp.concatenate([jnp.where(lane < QK_HEAD_DIM, q, zero),
                                        jnp.where(lane >= QK_HEAD_DIM, q, zero)], axis=0)

        def chunk(c):
            cols = slice(c * KEY_CHUNK, (c + 1) * KEY_CHUNK)
            s_ref[:, cols] = lax.dot_general(st["qq"], units[n][1](c), nt,
                                             preferred_element_type=F32)
        return [first] + [functools.partial(chunk, c) for c in range(n_chunks)]

    def e_stage(n):
        s_ref, p_ref = buf(n)
        dens = state[n]["dens"] = [None] * n_blocks

        def block(r):
            rows = slice(r * ROW_BLOCK, (r + 1) * ROW_BLOCK)
            tile = lambda t: s_ref[rows, t * LANES:(t + 1) * LANES]
            m = functools.reduce(jnp.maximum, [tile(t) for t in range(n_tiles)])
            m = jnp.broadcast_to(jnp.max(m, axis=-1, keepdims=True), (ROW_BLOCK, LANES))
            d = None
            for t in range(n_tiles):
                e = jnp.exp2(tile(t) - m)
                d = e if d is None else d + e
                p_ref[rows, t * LANES:(t + 1) * LANES] = e.astype(BF16)
            dens[r] = jnp.sum(d, axis=-1, keepdims=True)
        return [functools.partial(block, r) for r in range(n_blocks)]

    def v_stage(n):
        p_ref = buf(n)[1]
        st = state[n]

        def first():
            dens = st["dens"]
            st["d1"] = jnp.concatenate(dens[:n_blocks // 2], axis=0)
            d2 = jnp.concatenate(dens[n_blocks // 2:], axis=0)
            st["c"] = (lam * st["d1"] / d2).astype(BF16)
            st["o"] = jnp.zeros((ATT_ROWS, V_HEAD_DIM), F32)

        def chunk(c):
            cols = slice(c * KEY_CHUNK, (c + 1) * KEY_CHUNK)
            w = p_ref[0:ATT_ROWS, cols] - st["c"] * p_ref[ATT_ROWS:2 * ATT_ROWS, cols]
            st["o"] = st["o"] + _dot(w, units[n][2](c))

        def last():
            o = st["o"] * (1.0 / st["d1"])
            y = o * lax.rsqrt(jnp.mean(o * o, axis=-1, keepdims=True) + LN_EPS)
            units[n][4]((y * units[n][3]()) * (1.0 - lam_init))
        return [first] + [functools.partial(chunk, c) for c in range(n_chunks)] + [last]

    for t in range(n_groups + 2):
        stages = []
        for stage, g in ((q_stage, t), (e_stage, t - 1), (v_stage, t - 2)):
            if 0 <= g < n_groups:
                stages += [stage(n) for n in range(g * group, (g + 1) * group)]
        _interleave(*stages)


def _attention_units_keys_on_rows(units, lam, lam_init, bufs, n_keys, group):
    n_chunks = n_keys // KEY_CHUNK
    width = 2 * ATT_ROWS
    nt = (((1,), (1,)), ((), ()))
    tn = (((0,), (0,)), ((), ()))
    state = [dict() for _ in units]
    n_groups = len(units) // group

    def fold(x, op):
        return functools.reduce(op, [x[r:r + SUBLANES, :] for r in range(0, KEY_CHUNK, SUBLANES)])

    def buf(n):
        return bufs[(n // group) % 2][n % group]

    def q_stage(n):
        s_ref = buf(n)[0]
        st = state[n]

        def first():
            q = units[n][0]()
            lane = lax.broadcasted_iota(jnp.int32, q.shape, 1)
            zero = jnp.zeros_like(q)
            st["qq"] = jnp.concatenate([jnp.where(lane < QK_HEAD_DIM, q, zero),
                                        jnp.where(lane >= QK_HEAD_DIM, q, zero)], axis=0)
            st["m"] = None

        def chunk(c):
            s = lax.dot_general(units[n][1](c), st["qq"], nt, preferred_element_type=F32)
            s_ref[c * KEY_CHUNK:(c + 1) * KEY_CHUNK, :] = s
            m = fold(s, jnp.maximum)
            st["m"] = m if st["m"] is None else jnp.maximum(st["m"], m)
        return [first] + [functools.partial(chunk, c) for c in range(n_chunks)]

    def e_stage(n):
        s_ref, p_ref = buf(n)
        st = state[n]

        def first():
            st["m_row"] = jnp.max(st["m"], axis=0, keepdims=True)
            st["d"] = None

        def chunk(c):
            rows = slice(c * KEY_CHUNK, (c + 1) * KEY_CHUNK)
            e = jnp.exp2(s_ref[rows, :] - st["m_row"])
            d = fold(e, jnp.add)
            st["d"] = d if st["d"] is None else st["d"] + d
            p_ref[rows, :] = e.astype(BF16)
        return [first] + [functools.partial(chunk, c) for c in range(n_chunks)]

    def v_stage(n):
        p_ref = buf(n)[1]
        st = state[n]

        def first():
            den = jnp.sum(st["d"], axis=0, keepdims=True)
            st["a"] = (1.0 / den[:, :ATT_ROWS]).astype(BF16)
            st["b"] = (lam / den[:, ATT_ROWS:]).astype(BF16)
            st["o"] = jnp.zeros((ATT_ROWS, V_HEAD_DIM), F32)

        def chunk(c):
            rows = slice(c * KEY_CHUNK, (c + 1) * KEY_CHUNK)
            w = p_ref[rows, 0:ATT_ROWS] * st["a"] - p_ref[rows, ATT_ROWS:width] * st["b"]
            st["o"] = st["o"] + lax.dot_general(w, units[n][2](c), tn,
                                                preferred_element_type=F32)

        def last():
            o = st["o"]
            y = o * lax.rsqrt(jnp.mean(o * o, axis=-1, keepdims=True) + LN_EPS)
            units[n][4]((y * units[n][3]()) * (1.0 - lam_init))
        return [first] + [functools.partial(chunk, c) for c in range(n_chunks)] + [last]

    for t in range(n_groups + 2):
        stages = []
        for stage, g in ((q_stage, t), (e_stage, t - 1), (v_stage, t - 2)):
            if 0 <= g < n_groups:
                stages += [stage(n) for n in range(g * group, (g + 1) * group)]
        _interleave(*stages)


def _lam_init(l):
    return 0.8 - 0.6 * math.exp(-0.3 * l)


def _lambda(lq_ref, lam_init):
    lq = lq_ref[...]
    return (jnp.exp(jnp.sum(lq[0:1] * lq[1:2], axis=-1, keepdims=True))
            - jnp.exp(jnp.sum(lq[2:3] * lq[3:4], axis=-1, keepdims=True)) + lam_init)


def _attn_ctx_kernel(lq_ref, g_ref, q_ref, k_ref, v_ref, o_ref, s_all, p_all, *, lam_init):
    def store(rows, cols, y):
        o_ref[rows, cols] = y.astype(o_ref.dtype)

    units = []
    for sq in range(CTX_SEQS_PER_STEP):
        rows = slice(sq * SEQ, (sq + 1) * SEQ)
        for hd in range(ATT_HEADS):
            cols = slice(hd * V_HEAD_DIM, (hd + 1) * V_HEAD_DIM)
            units.append((lambda rows=rows, cols=cols: q_ref[rows, cols],
                          lambda c, rows=rows, cols=cols: k_ref[rows, cols],
                          lambda c, rows=rows, cols=cols: v_ref[rows, cols],
                          lambda hd=hd: g_ref[hd],
                          functools.partial(store, rows, cols)))
    bufs = [[(s_all.at[par * ATT_HEADS + i], p_all.at[par * ATT_HEADS + i])
             for i in range(ATT_HEADS)] for par in range(2)]
    _attention_units_keys_on_rows(units, _lambda(lq_ref, lam_init), lam_init, bufs, SEQ,
                                  group=ATT_HEADS)


def _attention_ctx(q, k, v, p, l):
    rows = CTX_SEQS_PER_STEP * SEQ
    blk = pl.BlockSpec((rows, ATT_WIDTH), lambda b: (b, 0))
    s_buf = pltpu.VMEM((2 * ATT_HEADS, SEQ, 2 * ATT_ROWS), F32)
    p_buf = pltpu.VMEM((2 * ATT_HEADS, SEQ, 2 * ATT_ROWS), BF16)
    return pl.pallas_call(
        functools.partial(_attn_ctx_kernel, lam_init=_lam_init(l)),
        grid=(BATCH // CTX_SEQS_PER_STEP,),
        in_specs=[
            _fixed_spec((4, QK_HEAD_DIM), l),
            _fixed_spec((ATT_HEADS, 1, V_HEAD_DIM), l),
            blk, blk, blk,
        ],
        out_specs=blk,
        out_shape=jax.ShapeDtypeStruct((N_CTX_TOK, ATT_WIDTH), BF16),
        scratch_shapes=[s_buf, p_buf],
        compiler_params=_params(1),
        name="attention_ctx",
    )(p["lambda_qk"], p["subln_g"], q, k, v)


def _attn_dec_kernel(lq_ref, g_ref, q_ref, pk_ref, pv_ref, k_ref, v_ref, o_ref,
                     kcat, vcat, s0, s1, p0, p1, *, lam_init):
    @pl.when(pl.program_id(2) == 0)
    def _():
        kcat[0:PAST_LEN, :] = pk_ref[...].astype(BF16)
        kcat[PAST_LEN:PAST_LEN + DEC_SEQ, :] = k_ref[...]
        vcat[0:PAST_LEN, :] = pv_ref[...].astype(BF16)
        vcat[PAST_LEN:PAST_LEN + DEC_SEQ, :] = v_ref[...]

    def store(rows, y):
        o_ref[rows, :] = y.astype(o_ref.dtype)

    def chunk_of(ref):
        return lambda c: ref[c * KEY_CHUNK:(c + 1) * KEY_CHUNK, :]

    units = []
    for r0 in range(0, DEC_TQ, ATT_ROWS):
        rows = slice(r0, r0 + ATT_ROWS)
        units.append((lambda rows=rows: q_ref[rows, :], chunk_of(kcat), chunk_of(vcat),
                      lambda: g_ref[0], functools.partial(store, rows)))
    _attention_units(units, _lambda(lq_ref, lam_init), lam_init, [[(s0, p0)], [(s1, p1)]],
                     PAST_LEN + DEC_SEQ, group=1)


def _attention_dec(q, k, v, p, l):
    nq = DEC_SEQ // DEC_TQ
    q0 = N_CTX_TOK // DEC_TQ
    s0 = N_CTX_TOK // DEC_SEQ
    n_keys = PAST_LEN + DEC_SEQ
    past = pl.BlockSpec((None, None, PAST_LEN, V_HEAD_DIM), lambda b, h, j: (b, l, 0, h))
    own = pl.BlockSpec((DEC_SEQ, V_HEAD_DIM), lambda b, h, j: (s0 + b, h))
    kv_buf = pltpu.VMEM((n_keys, V_HEAD_DIM), BF16)
    s_buf = pltpu.VMEM((2 * ATT_ROWS, n_keys), F32)
    p_buf = pltpu.VMEM((2 * ATT_ROWS, n_keys), BF16)
    return pl.pallas_call(
        functools.partial(_attn_dec_kernel, lam_init=_lam_init(l)),
        grid=(DEC_BATCH, ATT_HEADS, nq),
        in_specs=[
            _fixed_spec((4, QK_HEAD_DIM), l),
            pl.BlockSpec((None, 1, 1, V_HEAD_DIM), lambda b, h, j: (l, h, 0, 0)),
            pl.BlockSpec((DEC_TQ, V_HEAD_DIM), lambda b, h, j: (q0 + b * nq + j, h)),
            past, past, own, own,
        ],
        out_specs=pl.BlockSpec((DEC_TQ, V_HEAD_DIM), lambda b, h, j: (b * nq + j, h)),
        out_shape=jax.ShapeDtypeStruct((N_DEC_TOK, ATT_WIDTH), BF16),
        scratch_shapes=[kv_buf, kv_buf, s_buf, s_buf, p_buf, p_buf],
        compiler_params=_params(3),
        name="attention_dec",
    )(p["lambda_qk"], p["subln_g"], q, p["past_k"], p["past_v"], k, v)


def _local_kernel(u_ref, pu_ref, cw_ref, cb_ref, g_ref, b_ref, cu_ref, pd_ref,
                  upad, ppad, s2, s4, s8, shifted, *, seq):
    zpad = jnp.zeros((PAD, CONV_WIDTH), F32)
    for buf in (upad, ppad, s2, s4, s8):
        buf[0:PAD, :] = zpad
        buf[PAD + seq:PAD + seq + PAD, :] = zpad
    upad[PAD:PAD + seq, :] = u_ref[...]
    ppad[PAD:PAD + seq, :] = pu_ref[...]

    n_chunks = seq // LOCAL_CHUNK
    r = LOCAL_CHUNK
    centre = CONV_KERNEL // 2
    n_shift_groups = CONV_SHIFT_GROUPS
    halo = CONV_HALO

    for c in range(n_chunks):
        win0 = c * r
        for shift in range(1, SUBLANES):
            shifted[shift] = upad[win0 + shift:win0 + shift + r + halo, :]
        acc = jnp.zeros((r, CONV_WIDTH), F32)
        for shift in range(SUBLANES):
            for a in range(n_shift_groups):
                t = SUBLANES * a + shift - (PAD - centre)
                if 0 <= t < CONV_KERNEL:
                    lo = SUBLANES * a
                    window = (shifted[shift, lo:lo + r, :] if shift
                              else upad[win0 + lo:win0 + lo + r, :])
                    acc = acc + window * cw_ref[t:t + 1, :]
        y = _layer_norm(acc + cb_ref[...], g_ref[...], b_ref[...])
        cu_ref[c * r:(c + 1) * r, :] = jax.nn.silu(y).astype(cu_ref.dtype)

    ext = seq + PAD
    s2[8:8 + ext, :] = ppad[7:7 + ext, :] + ppad[8:8 + ext, :]
    s4[8:8 + ext, :] = s2[7:7 + ext, :] + s2[9:9 + ext, :]
    s8[8:8 + ext, :] = s4[6:6 + ext, :] + s4[10:10 + ext, :]
    lane = lax.broadcasted_iota(jnp.int32, (r, POOL_WIDTH), 1)
    grp = lane // POOL_GROUP_DIM
    half_win = jnp.where(grp == 0, 1, jnp.where(grp == 1, 2, jnp.where(grp == 2, 4, 8)))
    for c in range(n_chunks):
        base = PAD + c * r
        sl = slice(base, base + r)
        s16 = s8[base - 4:base - 4 + r, :] + s8[base + 4:base + 4 + r, :]
        tot = jnp.where(grp == 0, s2[sl, :],
                        jnp.where(grp == 1, s4[sl, :], jnp.where(grp == 2, s8[sl, :], s16)))
        t = lax.broadcasted_iota(jnp.int32, (r, POOL_WIDTH), 0) + c * r
        cnt = jnp.clip(t + half_win, 0, seq) - jnp.clip(t - half_win, 0, seq)
        pd_ref[c * r:(c + 1) * r, :] = (tot / cnt.astype(F32) - ppad[sl, :]).astype(pd_ref.dtype)


def _local(u, pu, p, l, *, n_seq, seq, row0):
    s0 = row0 // seq
    blk = pl.BlockSpec((seq, CONV_WIDTH), lambda b: (s0 + b, 0))
    out = pl.BlockSpec((seq, CONV_WIDTH), lambda b: (b, 0))
    vec = _fixed_spec((1, CONV_WIDTH), l)
    pad_buf = pltpu.VMEM((seq + 2 * PAD, CONV_WIDTH), F32)
    return pl.pallas_call(
        functools.partial(_local_kernel, seq=seq),
        grid=(n_seq,),
        in_specs=[blk, blk, _fixed_spec((CONV_KERNEL, CONV_WIDTH), l), vec, vec, vec],
        out_specs=[out, out],
        out_shape=[jax.ShapeDtypeStruct((n_seq * seq, CONV_WIDTH), BF16),
                   jax.ShapeDtypeStruct((n_seq * seq, POOL_WIDTH), BF16)],
        scratch_shapes=[pad_buf] * 5 + [
            pltpu.VMEM((SUBLANES, LOCAL_CHUNK + CONV_HALO, CONV_WIDTH), F32)],
        compiler_params=_params(1),
        name="local_mixers",
    )(u, pu, p["conv_dw_w"], p["conv_dw_b"], p["conv_ln_g"], p["conv_ln_b"])


def _merge_kernel(x_ref, mod_ref, oc_ref, od_ref, cuc_ref, cud_ref, pdc_ref, pdd_ref,
                  wg_ref, bg_ref, wa_ref, wc_ref, wgrp_ref, ps_ref, wp_ref, wo_ref, bo_ref,
                  g_ref, b_ref, out_ref, merged_ref):
    is_ctx = pl.program_id(0) < N_CTX_TILES
    x = x_ref[...]
    h = (x * (1.0 + mod_ref[0, 4:5, :]) + mod_ref[0, 3:4, :]).astype(BF16)
    att = jnp.where(is_ctx, oc_ref[...], od_ref[...])
    cu = jnp.where(is_ctx, cuc_ref[...], cud_ref[...])
    pd = jnp.where(is_ctx, pdc_ref[...], pdd_ref[...])
    pooled = (_dot(pd, wgrp_ref[...]) * ps_ref[...]).astype(BF16)
    cw = 256
    for j in range(D_MODEL // cw):
        sl = slice(j * cw, (j + 1) * cw)
        branches = (_dot(att, wa_ref[:, sl]), _dot(cu, wc_ref[:, sl]), _dot(pooled, wp_ref[:, sl]))
        merged = None
        for n, br in enumerate(branches):
            gs = slice(n * D_MODEL + j * cw, n * D_MODEL + (j + 1) * cw)
            gate = jax.nn.sigmoid(_dot(h, wg_ref[:, gs]) + bg_ref[:, gs])
            merged = gate * br if merged is None else merged + gate * br
        merged_ref[:, sl] = merged.astype(BF16)
    mix = _dot(merged_ref[...], wo_ref[...]) + bo_ref[...]
    out_ref[...] = _layer_norm(ALPHA * x + mod_ref[0, 5:6, :] * mix, g_ref[...], b_ref[...])


def _merge(x, o, cu, pd, p, l):
    row = lambda w: pl.BlockSpec((TM, w), lambda i: (i, 0))
    return pl.pallas_call(
        _merge_kernel,
        grid=(N_TILES,),
        in_specs=[
            row(D_MODEL),
            _mod_spec(l),
            *_group_specs(ATT_WIDTH), *_group_specs(CONV_WIDTH), *_group_specs(POOL_WIDTH),
            _fixed_spec((D_MODEL, GATE_COLS), l), _fixed_spec((1, GATE_COLS), l),
            _fixed_spec((ATT_WIDTH, D_MODEL), l), _fixed_spec((CONV_WIDTH, D_MODEL), l),
            _fixed_spec((POOL_WIDTH, POOL_WIDTH), l), _fixed_spec((1, POOL_WIDTH), l),
            _fixed_spec((POOL_WIDTH, D_MODEL), l),
            _fixed_spec((D_MODEL, D_MODEL), l), _fixed_spec((1, D_MODEL), l),
            _fixed_spec((1, D_MODEL), l, 1), _fixed_spec((1, D_MODEL), l, 1),
        ],
        out_specs=row(D_MODEL),
        out_shape=jax.ShapeDtypeStruct((N_TOK, D_MODEL), F32),
        scratch_shapes=[pltpu.VMEM((TM, D_MODEL), BF16)],
        compiler_params=_params(1),
        name="merge",
    )(x, p["mod"], *o, *cu, *pd, p["w_gate"], p["b_gate"], p["w_att_o"], p["w_conv_o"],
      p["w_pool_g"], p["pool_scale"], p["w_pool_o"], p["w_out"], p["b_out"], p["ln_g"], p["ln_b"])


def _block_diag(w_grp):
    eye = jnp.eye(POOL_GROUPS, dtype=w_grp.dtype)
    return jnp.einsum("lgcd,gh->lgchd", w_grp, eye).reshape(-1, POOL_WIDTH, POOL_WIDTH)


def kernel(x_prompt, x_sample, cache_k, cache_v, c, c_ctx, w_mod, b_mod, w_ffn_in, w_ffn_out, ln_g, ln_b, w_in, b_in, lambda_qk, subln_g, w_att_o, conv_dw_w, conv_dw_b, conv_ln_g, conv_ln_b, w_conv_o, w_pool_g, pool_scale, w_pool_o, w_out, b_out):
    x = (x_prompt.reshape(N_CTX_TOK, D_MODEL), x_sample.reshape(N_DEC_TOK, D_MODEL))
    cond = jnp.zeros((COND_ROWS, D_MODEL), F32).at[0].set(c_ctx).at[1:1 + DEC_BATCH].set(c)
    rows = lambda a: a.reshape(a.shape[:-1] + (1, a.shape[-1]))
    p = dict(
        mod=_modulation(cond, w_mod, b_mod).reshape(DEPTH, COND_ROWS, N_MOD, D_MODEL),
        rope=_rope_tables(),
        past_k=cache_k.reshape(DEC_BATCH, DEPTH, PAST_LEN, QK_COLS),
        past_v=cache_v.reshape(DEC_BATCH, DEPTH, PAST_LEN, ATT_WIDTH),
        w_ffn_in=w_ffn_in.astype(BF16), w_ffn_out=w_ffn_out.astype(BF16),
        ln_g=rows(ln_g), ln_b=rows(ln_b),
        w_proj=w_in[:, :, :PROJ_COLS].astype(BF16), b_in=rows(b_in),
        w_gate=w_in[:, :, PROJ_COLS:].astype(BF16), b_gate=rows(b_in[:, PROJ_COLS:]),
        lambda_qk=lambda_qk, subln_g=rows(subln_g),
        conv_dw_w=conv_dw_w, conv_dw_b=rows(conv_dw_b),
        conv_ln_g=rows(conv_ln_g), conv_ln_b=rows(conv_ln_b),
        w_att_o=w_att_o.astype(BF16), w_conv_o=w_conv_o.astype(BF16),
        w_pool_g=_block_diag(w_pool_g).astype(BF16), pool_scale=rows(pool_scale),
        w_pool_o=w_pool_o.astype(BF16), w_out=w_out.astype(BF16), b_out=rows(b_out),
    )

    new_k, new_v = [], []
    for l in range(DEPTH):
        x = _ffn(x, p, l, sub=0)
        q, k, v, u, pu, kf, vf = _inproj(x, p, l)
        new_k.append(kf.reshape(BATCH, SEQ, ATT_HEADS, 2, QK_HEAD_DIM))
        new_v.append(vf.reshape(BATCH, SEQ, ATT_HEADS, V_HEAD_DIM))
        o = (_attention_ctx(q, k, v, p, l), _attention_dec(q, k, v, p, l))
        cu_ctx, pd_ctx = _local(u, pu, p, l, n_seq=BATCH, seq=SEQ, row0=0)
        cu_dec, pd_dec = _local(u, pu, p, l, n_seq=DEC_BATCH, seq=DEC_SEQ, row0=N_CTX_TOK)
        x = _merge(x, o, (cu_ctx, cu_dec), (pd_ctx, pd_dec), p, l)
        x = _ffn(x, p, l, sub=2, split_out=(l == DEPTH - 1))

    y_prompt = x[0].reshape(BATCH, SEQ, D_MODEL)
    y_sample = x[1].reshape(DEC_BATCH, DEC_SEQ, D_MODEL)
    return (y_prompt, y_sample, jnp.stack(new_k, axis=1), jnp.stack(new_v, axis=1))
```

```python
import functools
import math

import numpy as np
import jax
import jax.numpy as jnp
from jax import lax
from jax.experimental import pallas as pl
from jax.experimental.pallas import tpu as pltpu

F32 = jnp.float32
BF16 = jnp.bfloat16

D_MODEL = 1024
BATCH = 32
SEQ = 256
DEPTH = 4
DEC_BATCH = 8
DEC_SEQ = 2048
PAST_LEN = 512
GRID_W = 64
ATT_HEADS = 4
QK_HEAD_DIM = 64
V_HEAD_DIM = 2 * QK_HEAD_DIM
ATT_WIDTH = ATT_HEADS * V_HEAD_DIM
QK_COLS = ATT_HEADS * 2 * QK_HEAD_DIM
ROPE_BASE = 10000.0
CONV_WIDTH = D_MODEL // 4
CONV_KERNEL = 31
POOL_WIDTH = D_MODEL // 4
POOL_WINDOWS = (2, 4, 8, 16)
POOL_GROUPS = 4
POOL_GROUP_DIM = POOL_WIDTH // POOL_GROUPS
N_BRANCH = 3
D_FF = ((8 * D_MODEL // 3 + 127) // 128) * 128
N_MOD = 9
ALPHA = (2 * DEPTH) ** 0.25
LN_EPS = 1e-5

PROJ_COLS = 2 * QK_COLS + ATT_WIDTH + 2 * CONV_WIDTH + POOL_WIDTH
GATE_COLS = N_BRANCH * D_MODEL

N_CTX_TOK = BATCH * SEQ
N_DEC_TOK = DEC_BATCH * DEC_SEQ
N_TOK = N_CTX_TOK + N_DEC_TOK

SUBLANES = 8
LANES = 128
TM = 1024
N_CTX_TILES = N_CTX_TOK // TM
DEC_TILES_PER_SEQ = DEC_SEQ // TM
N_TILES = N_TOK // TM
COND_ROWS = 16
FF_CHUNK = 256
STAGE_ROWS = 128
STAGE_ROWS_WIDE = 32
MOD_TN = 1536
ATT_ROWS = 256
KEY_CHUNK = 256
ROW_BLOCK = 64
DEC_TQ = 8 * ATT_ROWS
CTX_SEQS_PER_STEP = 2
PAD = 16
LOCAL_CHUNK = 128
CONV_SHIFT_GROUPS = -(-(CONV_KERNEL + 1) // SUBLANES)
CONV_HALO = (CONV_SHIFT_GROUPS - 1) * SUBLANES

Q_SCALE = QK_HEAD_DIM ** -0.5 * math.log2(math.e)

VMEM_LIMIT = 60 * 1024 * 1024


def _cond_row(i):
    return jnp.where(i < N_CTX_TILES, 0, 1 + (i - N_CTX_TILES) // DEC_TILES_PER_SEQ)


def _ctx_tile(i):
    return (jnp.minimum(i, N_CTX_TILES - 1), 0)


def _dec_tile(i):
    return (jnp.maximum(i - N_CTX_TILES, 0), 0)


def _group_specs(width):
    return [pl.BlockSpec((TM, width), _ctx_tile), pl.BlockSpec((TM, width), _dec_tile)]


def _layer_norm(r, g, b):
    mu = jnp.mean(r, axis=-1, keepdims=True)
    d = r - mu
    var = jnp.mean(d * d, axis=-1, keepdims=True)
    return d * lax.rsqrt(var + LN_EPS) * g + b


def _dot(a, b):
    return jnp.dot(a, b, preferred_element_type=F32)


def _fixed_spec(tail, *lead):
    index = tuple(lead) + (0,) * len(tail)
    return pl.BlockSpec((None,) * len(lead) + tuple(tail), lambda *_: index,
                        pipeline_mode=pl.Buffered(1))


def _mod_spec(l):
    return pl.BlockSpec((None, 1, N_MOD, D_MODEL), lambda i: (l, _cond_row(i), 0, 0))


def _params(n_grid_dims):
    return pltpu.CompilerParams(dimension_semantics=("arbitrary",) * n_grid_dims,
                                vmem_limit_bytes=VMEM_LIMIT)


def _mod_kernel(cond_ref, w_ref, b_ref, o_ref):
    a = jax.nn.silu(cond_ref[...]).astype(BF16)
    o_ref[0] = _dot(a, w_ref[0].astype(BF16)) + b_ref[0]


def _modulation(cond, w_mod, b_mod):
    n_col = N_MOD * D_MODEL
    return pl.pallas_call(
        _mod_kernel,
        grid=(DEPTH, n_col // MOD_TN),
        in_specs=[
            pl.BlockSpec((COND_ROWS, D_MODEL), lambda l, j: (0, 0)),
            pl.BlockSpec((1, D_MODEL, MOD_TN), lambda l, j: (l, 0, j)),
            pl.BlockSpec((1, 1, MOD_TN), lambda l, j: (l, 0, j)),
        ],
        out_specs=pl.BlockSpec((1, COND_ROWS, MOD_TN), lambda l, j: (l, 0, j)),
        out_shape=jax.ShapeDtypeStruct((DEPTH, COND_ROWS, n_col), F32),
        compiler_params=_params(2),
        name="modulation",
    )(cond, w_mod, b_mod.reshape(DEPTH, 1, n_col))


def _stream_cast(src_rows, dst_ref, stage_ref, sem):
    n = stage_ref.shape[1]
    n_chunks = dst_ref.shape[0] // n

    def copy(c):
        return pltpu.make_async_copy(src_rows(c * n, n), stage_ref.at[c % 2], sem.at[c % 2])

    copy(0).start()
    for c in range(n_chunks):
        if c + 1 < n_chunks:
            copy(c + 1).start()
        copy(c).wait()
        dst_ref[c * n:(c + 1) * n, :] = stage_ref[c % 2].astype(BF16)


def _weight_scratch(rows, cols, stage_rows):
    return [pltpu.VMEM((rows, cols), BF16), pltpu.VMEM((2, stage_rows, cols), F32),
            pltpu.SemaphoreType.DMA((2,))]


_HBM = pl.BlockSpec(memory_space=pl.ANY)


def _ffn_kernel(*refs, l, sub, split_in, split_out):
    n_x = 2 if split_in else 1
    n_o = 2 if split_out else 1
    x_refs = refs[:n_x]
    mod_ref, win_hbm, wout_hbm, g_ref, b_ref = refs[n_x:n_x + 5]
    o_refs = refs[n_x + 5:n_x + 5 + n_o]
    act_ref, win_ref, win_stage, win_sem, wout_ref, wout_stage, wout_sem = refs[n_x + 5 + n_o:]

    @pl.when(pl.program_id(0) == 0)
    def _():
        _stream_cast(lambda r0, n: win_hbm.at[l, sub // 2, pl.ds(r0, n), :],
                     win_ref, win_stage, win_sem)
        _stream_cast(lambda r0, n: wout_hbm.at[l, sub // 2, pl.ds(r0, n), :],
                     wout_ref, wout_stage, wout_sem)

    is_ctx = pl.program_id(0) < N_CTX_TILES
    x = jnp.where(is_ctx, x_refs[0][...], x_refs[1][...]) if split_in else x_refs[0][...]
    shift = mod_ref[0, 3 * sub:3 * sub + 1, :]
    scale = mod_ref[0, 3 * sub + 1:3 * sub + 2, :]
    gate = mod_ref[0, 3 * sub + 2:3 * sub + 3, :]
    h = (x * (1.0 + scale) + shift).astype(BF16)
    for j in range(D_FF // FF_CHUNK):
        lo = j * FF_CHUNK
        g = _dot(h, win_ref[:, lo:lo + FF_CHUNK])
        u = _dot(h, win_ref[:, D_FF + lo:D_FF + lo + FF_CHUNK])
        act_ref[:, lo:lo + FF_CHUNK] = (jax.nn.silu(g) * u).astype(BF16)
    y = _dot(act_ref[...], wout_ref[...])
    out = _layer_norm(ALPHA * x + 0.5 * gate * y, g_ref[...], b_ref[...])
    if split_out:
        @pl.when(is_ctx)
        def _():
            o_refs[0][...] = out

        @pl.when(jnp.logical_not(is_ctx))
        def _():
            o_refs[1][...] = out
    else:
        o_refs[0][...] = out


def _ffn(x, p, l, sub, split_out=False):
    split_in = isinstance(x, tuple)
    xs = x if split_in else (x,)
    row = pl.BlockSpec((TM, D_MODEL), lambda i: (i, 0))
    if split_out:
        out_specs = _group_specs(D_MODEL)
        out_shape = [jax.ShapeDtypeStruct((N_CTX_TOK, D_MODEL), F32),
                     jax.ShapeDtypeStruct((N_DEC_TOK, D_MODEL), F32)]
    else:
        out_specs = row
        out_shape = jax.ShapeDtypeStruct((N_TOK, D_MODEL), F32)
    return pl.pallas_call(
        functools.partial(_ffn_kernel, l=l, sub=sub, split_in=split_in, split_out=split_out),
        grid=(N_TILES,),
        in_specs=(_group_specs(D_MODEL) if split_in else [row]) + [
            _mod_spec(l), _HBM, _HBM,
            _fixed_spec((1, D_MODEL), l, sub),
            _fixed_spec((1, D_MODEL), l, sub),
        ],
        out_specs=out_specs,
        out_shape=out_shape,
        scratch_shapes=[pltpu.VMEM((TM, D_FF), BF16)]
        + _weight_scratch(D_MODEL, 2 * D_FF, STAGE_ROWS_WIDE)
        + _weight_scratch(D_FF, D_MODEL, STAGE_ROWS),
        compiler_params=_params(1),
        name="ffn",
    )(*xs, p["mod"], p["w_ffn_in"], p["w_ffn_out"], p["ln_g"], p["ln_b"])


def _rope_tables():
    nf = QK_HEAD_DIM // 4
    t = np.arange(DEC_SEQ)
    lane = np.arange(V_HEAD_DIM)
    d = lane % QK_HEAD_DIM
    by_col = d >= QK_HEAD_DIM // 2
    e = d % (QK_HEAD_DIM // 2)
    inv = ROPE_BASE ** (-(e % nf).astype(np.float64) / nf)
    pos = np.where(by_col[None, :], (t % GRID_W)[:, None], (t // GRID_W)[:, None])
    ang = pos.astype(np.float64) * inv[None, :]
    first = (e < nf)[None, :]
    cos = np.cos(ang)
    sin_next = np.where(first, -np.sin(ang), 0.0)
    sin_prev = np.where(first, 0.0, np.sin(ang))
    ident = np.zeros((TM, V_HEAD_DIM))
    cos = np.concatenate([cos, ident + 1.0], axis=0)
    sin_next = np.concatenate([sin_next, ident], axis=0)
    sin_prev = np.concatenate([sin_prev, ident], axis=0)
    return (jnp.asarray(cos, F32), jnp.asarray(sin_next, F32), jnp.asarray(sin_prev, F32))


def _inproj_kernel(x_ref, mod_ref, w_hbm, b_ref, cos_ref, sn_ref, sp_ref,
                   q_ref, k_ref, v_ref, u_ref, pu_ref, kf_ref, vf_ref,
                   w_ref, w_stage, w_sem, *, l):
    @pl.when(pl.program_id(0) == 0)
    def _():
        _stream_cast(lambda r0, n: w_hbm.at[l, pl.ds(r0, n), pl.ds(0, PROJ_COLS)],
                     w_ref, w_stage, w_sem)

    is_ctx = pl.program_id(0) < N_CTX_TILES
    x = x_ref[...]
    h = (x * (1.0 + mod_ref[0, 4:5, :]) + mod_ref[0, 3:4, :]).astype(BF16)
    cos = cos_ref[...]
    sn = sn_ref[...]
    sp = sp_ref[...]
    half = QK_HEAD_DIM // 4

    def proj(lo, width):
        return _dot(h, w_ref[:, lo:lo + width]) + b_ref[:, lo:lo + width]

    def rope(z):
        up = pltpu.roll(z, V_HEAD_DIM - half, axis=1)
        dn = pltpu.roll(z, half, axis=1)
        return z * cos + up * sn + dn * sp

    zq = proj(0, QK_COLS)
    zk = proj(QK_COLS, QK_COLS)
    for hd in range(ATT_HEADS):
        sl = slice(hd * V_HEAD_DIM, (hd + 1) * V_HEAD_DIM)
        q_ref[:, sl] = (rope(zq[:, sl]) * Q_SCALE).astype(BF16)
        k_ref[:, sl] = rope(zk[:, sl]).astype(BF16)
    zv = proj(2 * QK_COLS, ATT_WIDTH)
    v_ref[...] = zv.astype(BF16)

    @pl.when(is_ctx)
    def _():
        kf_ref[...] = zk
        vf_ref[...] = zv

    c1 = 2 * QK_COLS + ATT_WIDTH
    zc = proj(c1, 2 * CONV_WIDTH)
    u_ref[...] = zc[:, :CONV_WIDTH] * jax.nn.sigmoid(zc[:, CONV_WIDTH:])
    pu_ref[...] = proj(c1 + 2 * CONV_WIDTH, POOL_WIDTH)


def _inproj(x, p, l):
    cos, sn, sp = p["rope"]

    def tab_map(i):
        return (jnp.where(i < N_CTX_TILES, DEC_TILES_PER_SEQ,
                          (i - N_CTX_TILES) % DEC_TILES_PER_SEQ), 0)

    tab_spec = pl.BlockSpec((TM, V_HEAD_DIM), tab_map)
    row = lambda w: pl.BlockSpec((TM, w), lambda i: (i, 0))
    return pl.pallas_call(
        functools.partial(_inproj_kernel, l=l),
        grid=(N_TILES,),
        in_specs=[
            row(D_MODEL),
            _mod_spec(l), _HBM,
            _fixed_spec((1, PROJ_COLS), l),
            tab_spec, tab_spec, tab_spec,
        ],
        out_specs=[
            row(QK_COLS), row(QK_COLS), row(ATT_WIDTH), row(CONV_WIDTH), row(POOL_WIDTH),
            pl.BlockSpec((TM, QK_COLS), _ctx_tile),
            pl.BlockSpec((TM, ATT_WIDTH), _ctx_tile),
        ],
        out_shape=[
            jax.ShapeDtypeStruct((N_TOK, QK_COLS), BF16),
            jax.ShapeDtypeStruct((N_TOK, QK_COLS), BF16),
            jax.ShapeDtypeStruct((N_TOK, ATT_WIDTH), BF16),
            jax.ShapeDtypeStruct((N_TOK, CONV_WIDTH), F32),
            jax.ShapeDtypeStruct((N_TOK, POOL_WIDTH), F32),
            jax.ShapeDtypeStruct((N_CTX_TOK, QK_COLS), F32),
            jax.ShapeDtypeStruct((N_CTX_TOK, ATT_WIDTH), F32),
        ],
        scratch_shapes=_weight_scratch(D_MODEL, PROJ_COLS, STAGE_ROWS_WIDE),
        compiler_params=_params(1),
        name="inproj",
    )(x, p["mod"], p["w_in"], p["b_in"], cos, sn, sp)


def _interleave(*stages):
    for i in range(max(len(st) for st in stages)):
        for st in stages:
            if i < len(st):
                st[i]()


def _attention_units(units, lam, lam_init, bufs, n_keys, group):
    n_chunks = n_keys // KEY_CHUNK
    n_tiles = n_keys // LANES
    n_blocks = 2 * ATT_ROWS // ROW_BLOCK
    nt = (((1,), (1,)), ((), ()))
    state = [dict() for _ in units]
    n_groups = len(units) // group

    def buf(n):
        return bufs[(n // group) % 2][n % group]

    def q_stage(n):
        s_ref = buf(n)[0]
        st = state[n]

        def first():
            q = units[n][0]()
            lane = lax.broadcasted_iota(jnp.int32, q.shape, 1)
            zero = jnp.zeros_like(q)
            st["qq"] = jnp.concatenate([jnp.where(lane < QK_HEAD_DIM, q, zero),
                                        jnp.where(lane >= QK_HEAD_DIM, q, zero)], axis=0)

        def chunk(c):
            cols = slice(c * KEY_CHUNK, (c + 1) * KEY_CHUNK)
            s_ref[:, cols] = lax.dot_general(st["qq"], units[n][1](c), nt,
                                             preferred_element_type=F32)
        return [first] + [functools.partial(chunk, c) for c in range(n_chunks)]

    def e_stage(n):
        s_ref, p_ref = buf(n)
        dens = state[n]["dens"] = [None] * n_blocks

        def block(r):
            rows = slice(r * ROW_BLOCK, (r + 1) * ROW_BLOCK)
            tile = lambda t: s_ref[rows, t * LANES:(t + 1) * LANES]
            m = functools.reduce(jnp.maximum, [tile(t) for t in range(n_tiles)])
            m = jnp.broadcast_to(jnp.max(m, axis=-1, keepdims=True), (ROW_BLOCK, LANES))
            d = None
            for t in range(n_tiles):
                e = jnp.exp2(tile(t) - m)
                d = e if d is None else d + e
                p_ref[rows, t * LANES:(t + 1) * LANES] = e.astype(BF16)
            dens[r] = jnp.sum(d, axis=-1, keepdims=True)
        return [functools.partial(block, r) for r in range(n_blocks)]

    def v_stage(n):
        p_ref = buf(n)[1]
        st = state[n]

        def first():
            dens = st["dens"]
            st["d1"] = jnp.concatenate(dens[:n_blocks // 2], axis=0)
            d2 = jnp.concatenate(dens[n_blocks // 2:], axis=0)
            st["c"] = (lam * st["d1"] / d2).astype(BF16)
            st["o"] = jnp.zeros((ATT_ROWS, V_HEAD_DIM), F32)

        def chunk(c):
            cols = slice(c * KEY_CHUNK, (c + 1) * KEY_CHUNK)
            w = p_ref[0:ATT_ROWS, cols] - st["c"] * p_ref[ATT_ROWS:2 * ATT_ROWS, cols]
            st["o"] = st["o"] + _dot(w, units[n][2](c))

        def last():
            o = st["o"] * (1.0 / st["d1"])
            y = o * lax.rsqrt(jnp.mean(o * o, axis=-1, keepdims=True) + LN_EPS)
            units[n][4]((y * units[n][3]()) * (1.0 - lam_init))
        return [first] + [functools.partial(chunk, c) for c in range(n_chunks)] + [last]

    for t in range(n_groups + 2):
        stages = []
        for stage, g in ((q_stage, t), (e_stage, t - 1), (v_stage, t - 2)):
            if 0 <= g < n_groups:
                stages += [stage(n) for n in range(g * group, (g + 1) * group)]
        _interleave(*stages)


def _attention_units_keys_on_rows(units, lam, lam_init, bufs, n_keys, group):
    n_chunks = n_keys // KEY_CHUNK
    width = 2 * ATT_ROWS
    nt = (((1,), (1,)), ((), ()))
    tn = (((0,), (0,)), ((), ()))
    state = [dict() for _ in units]
    n_groups = len(units) // group

    def fold(x, op):
        return functools.reduce(op, [x[r:r + SUBLANES, :] for r in range(0, KEY_CHUNK, SUBLANES)])

    def buf(n):
        return bufs[(n // group) % 2][n % group]

    def q_stage(n):
        s_ref = buf(n)[0]
        st = state[n]

        def first():
            q = units[n][0]()
            lane = lax.broadcasted_iota(jnp.int32, q.shape, 1)
            zero = jnp.zeros_like(q)
            st["qq"] = jnp.concatenate([jnp.where(lane < QK_HEAD_DIM, q, zero),
                                        jnp.where(lane >= QK_HEAD_DIM, q, zero)], axis=0)
            st["m"] = None

        def chunk(c):
            s = lax.dot_general(units[n][1](c), st["qq"], nt, preferred_element_type=F32)
            s_ref[c * KEY_CHUNK:(c + 1) * KEY_CHUNK, :] = s
            m = fold(s, jnp.maximum)
            st["m"] = m if st["m"] is None else jnp.maximum(st["m"], m)
        return [first] + [functools.partial(chunk, c) for c in range(n_chunks)]

    def e_stage(n):
        s_ref, p_ref = buf(n)
        st = state[n]

        def first():
            st["m_row"] = jnp.max(st["m"], axis=0, keepdims=True)
            st["d"] = None

        def chunk(c):
            rows = slice(c * KEY_CHUNK, (c + 1) * KEY_CHUNK)
            e = jnp.exp2(s_ref[rows, :] - st["m_row"])
            d = fold(e, jnp.add)
            st["d"] = d if st["d"] is None else st["d"] + d
            p_ref[rows, :] = e.astype(BF16)
        return [first] + [functools.partial(chunk, c) for c in range(n_chunks)]

    def v_stage(n):
        p_ref = buf(n)[1]
        st = state[n]

        def first():
            den = jnp.sum(st["d"], axis=0, keepdims=True)
            st["a"] = (1.0 / den[:, :ATT_ROWS]).astype(BF16)
            st["b"] = (lam / den[:, ATT_ROWS:]).astype(BF16)
            st["o"] = jnp.zeros((ATT_ROWS, V_HEAD_DIM), F32)

        def chunk(c):
            rows = slice(c * KEY_CHUNK, (c + 1) * KEY_CHUNK)
            w = p_ref[rows, 0:ATT_ROWS] * st["a"] - p_ref[rows, ATT_ROWS:width] * st["b"]
            st["o"] = st["o"] + lax.dot_general(w, units[n][2](c), tn,
                                                preferred_element_type=F32)

        def last():
            o = st["o"]
            y = o * lax.rsqrt(jnp.mean(o * o, axis=-1, keepdims=True) + LN_EPS)
            units[n][4]((y * units[n][3]()) * (1.0 - lam_init))
        return [first] + [functools.partial(chunk, c) for c in range(n_chunks)] + [last]

    for t in range(n_groups + 2):
        stages = []
        for stage, g in ((q_stage, t), (e_stage, t - 1), (v_stage, t - 2)):
            if 0 <= g < n_groups:
                stages += [stage(n) for n in range(g * group, (g + 1) * group)]
        _interleave(*stages)


def _lam_init(l):
    return 0.8 - 0.6 * math.exp(-0.3 * l)


def _lambda(lq_ref, lam_init):
    lq = lq_ref[...]
    return (jnp.exp(jnp.sum(lq[0:1] * lq[1:2], axis=-1, keepdims=True))
            - jnp.exp(jnp.sum(lq[2:3] * lq[3:4], axis=-1, keepdims=True)) + lam_init)


def _attn_ctx_kernel(lq_ref, g_ref, q_ref, k_ref, v_ref, o_ref, s_all, p_all, *, lam_init):
    def store(rows, cols, y):
        o_ref[rows, cols] = y.astype(o_ref.dtype)

    units = []
    for sq in range(CTX_SEQS_PER_STEP):
        rows = slice(sq * SEQ, (sq + 1) * SEQ)
        for hd in range(ATT_HEADS):
            cols = slice(hd * V_HEAD_DIM, (hd + 1) * V_HEAD_DIM)
            units.append((lambda rows=rows, cols=cols: q_ref[rows, cols],
                          lambda c, rows=rows, cols=cols: k_ref[rows, cols],
                          lambda c, rows=rows, cols=cols: v_ref[rows, cols],
                          lambda hd=hd: g_ref[hd],
                          functools.partial(store, rows, cols)))
    bufs = [[(s_all.at[par * ATT_HEADS + i], p_all.at[par * ATT_HEADS + i])
             for i in range(ATT_HEADS)] for par in range(2)]
    _attention_units_keys_on_rows(units, _lambda(lq_ref, lam_init), lam_init, bufs, SEQ,
                                  group=ATT_HEADS)


def _attention_ctx(q, k, v, p, l):
    rows = CTX_SEQS_PER_STEP * SEQ
    blk = pl.BlockSpec((rows, ATT_WIDTH), lambda b: (b, 0))
    s_buf = pltpu.VMEM((2 * ATT_HEADS, SEQ, 2 * ATT_ROWS), F32)
    p_buf = pltpu.VMEM((2 * ATT_HEADS, SEQ, 2 * ATT_ROWS), BF16)
    return pl.pallas_call(
        functools.partial(_attn_ctx_kernel, lam_init=_lam_init(l)),
        grid=(BATCH // CTX_SEQS_PER_STEP,),
        in_specs=[
            _fixed_spec((4, QK_HEAD_DIM), l),
            _fixed_spec((ATT_HEADS, 1, V_HEAD_DIM), l),
            blk, blk, blk,
        ],
        out_specs=blk,
        out_shape=jax.ShapeDtypeStruct((N_CTX_TOK, ATT_WIDTH), BF16),
        scratch_shapes=[s_buf, p_buf],
        compiler_params=_params(1),
        name="attention_ctx",
    )(p["lambda_qk"], p["subln_g"], q, k, v)


def _attn_dec_kernel(lq_ref, g_ref, q_ref, pk_ref, pv_ref, k_ref, v_ref, o_ref,
                     kcat, vcat, s0, s1, p0, p1, *, lam_init):
    @pl.when(pl.program_id(2) == 0)
    def _():
        kcat[0:PAST_LEN, :] = pk_ref[...].astype(BF16)
        kcat[PAST_LEN:PAST_LEN + DEC_SEQ, :] = k_ref[...]
        vcat[0:PAST_LEN, :] = pv_ref[...].astype(BF16)
        vcat[PAST_LEN:PAST_LEN + DEC_SEQ, :] = v_ref[...]

    def store(rows, y):
        o_ref[rows, :] = y.astype(o_ref.dtype)

    def chunk_of(ref):
        return lambda c: ref[c * KEY_CHUNK:(c + 1) * KEY_CHUNK, :]

    units = []
    for r0 in range(0, DEC_TQ, ATT_ROWS):
        rows = slice(r0, r0 + ATT_ROWS)
        units.append((lambda rows=rows: q_ref[rows, :], chunk_of(kcat), chunk_of(vcat),
                      lambda: g_ref[0], functools.partial(store, rows)))
    _attention_units(units, _lambda(lq_ref, lam_init), lam_init, [[(s0, p0)], [(s1, p1)]],
                     PAST_LEN + DEC_SEQ, group=1)


def _attention_dec(q, k, v, p, l):
    nq = DEC_SEQ // DEC_TQ
    q0 = N_CTX_TOK // DEC_TQ
    s0 = N_CTX_TOK // DEC_SEQ
    n_keys = PAST_LEN + DEC_SEQ
    past = pl.BlockSpec((None, None, PAST_LEN, V_HEAD_DIM), lambda b, h, j: (b, l, 0, h))
    own = pl.BlockSpec((DEC_SEQ, V_HEAD_DIM), lambda b, h, j: (s0 + b, h))
    kv_buf = pltpu.VMEM((n_keys, V_HEAD_DIM), BF16)
    s_buf = pltpu.VMEM((2 * ATT_ROWS, n_keys), F32)
    p_buf = pltpu.VMEM((2 * ATT_ROWS, n_keys), BF16)
    return pl.pallas_call(
        functools.partial(_attn_dec_kernel, lam_init=_lam_init(l)),
        grid=(DEC_BATCH, ATT_HEADS, nq),
        in_specs=[
            _fixed_spec((4, QK_HEAD_DIM), l),
            pl.BlockSpec((None, 1, 1, V_HEAD_DIM), lambda b, h, j: (l, h, 0, 0)),
            pl.BlockSpec((DEC_TQ, V_HEAD_DIM), lambda b, h, j: (q0 + b * nq + j, h)),
            past, past, own, own,
        ],
        out_specs=pl.BlockSpec((DEC_TQ, V_HEAD_DIM), lambda b, h, j: (b * nq + j, h)),
        out_shape=jax.ShapeDtypeStruct((N_DEC_TOK, ATT_WIDTH), BF16),
        scratch_shapes=[kv_buf, kv_buf, s_buf, s_buf, p_buf, p_buf],
        compiler_params=_params(3),
        name="attention_dec",
    )(p["lambda_qk"], p["subln_g"], q, p["past_k"], p["past_v"], k, v)


def _local_kernel(u_ref, pu_ref, cw_ref, cb_ref, g_ref, b_ref, cu_ref, pd_ref,
                  upad, ppad, s2, s4, s8, shifted, *, seq):
    zpad = jnp.zeros((PAD, CONV_WIDTH), F32)
    for buf in (upad, ppad, s2, s4, s8):
        buf[0:PAD, :] = zpad
        buf[PAD + seq:PAD + seq + PAD, :] = zpad
    upad[PAD:PAD + seq, :] = u_ref[...]
    ppad[PAD:PAD + seq, :] = pu_ref[...]

    n_chunks = seq // LOCAL_CHUNK
    r = LOCAL_CHUNK
    centre = CONV_KERNEL // 2
    n_shift_groups = CONV_SHIFT_GROUPS
    halo = CONV_HALO

    for c in range(n_chunks):
        win0 = c * r
        for shift in range(1, SUBLANES):
            shifted[shift] = upad[win0 + shift:win0 + shift + r + halo, :]
        acc = jnp.zeros((r, CONV_WIDTH), F32)
        for shift in range(SUBLANES):
            for a in range(n_shift_groups):
                t = SUBLANES * a + shift - (PAD - centre)
                if 0 <= t < CONV_KERNEL:
                    lo = SUBLANES * a
                    window = (shifted[shift, lo:lo + r, :] if shift
                              else upad[win0 + lo:win0 + lo + r, :])
                    acc = acc + window * cw_ref[t:t + 1, :]
        y = _layer_norm(acc + cb_ref[...], g_ref[...], b_ref[...])
        cu_ref[c * r:(c + 1) * r, :] = jax.nn.silu(y).astype(cu_ref.dtype)

    ext = seq + PAD
    s2[8:8 + ext, :] = ppad[7:7 + ext, :] + ppad[8:8 + ext, :]
    s4[8:8 + ext, :] = s2[7:7 + ext, :] + s2[9:9 + ext, :]
    s8[8:8 + ext, :] = s4[6:6 + ext, :] + s4[10:10 + ext, :]
    lane = lax.broadcasted_iota(jnp.int32, (r, POOL_WIDTH), 1)
    grp = lane // POOL_GROUP_DIM
    half_win = jnp.where(grp == 0, 1, jnp.where(grp == 1, 2, jnp.where(grp == 2, 4, 8)))
    for c in range(n_chunks):
        base = PAD + c * r
        sl = slice(base, base + r)
        s16 = s8[base - 4:base - 4 + r, :] + s8[base + 4:base + 4 + r, :]
        tot = jnp.where(grp == 0, s2[sl, :],
                        jnp.where(grp == 1, s4[sl, :], jnp.where(grp == 2, s8[sl, :], s16)))
        t = lax.broadcasted_iota(jnp.int32, (r, POOL_WIDTH), 0) + c * r
        cnt = jnp.clip(t + half_win, 0, seq) - jnp.clip(t - half_win, 0, seq)
        pd_ref[c * r:(c + 1) * r, :] = (tot / cnt.astype(F32) - ppad[sl, :]).astype(pd_ref.dtype)


def _local(u, pu, p, l, *, n_seq, seq, row0):
    s0 = row0 // seq
    blk = pl.BlockSpec((seq, CONV_WIDTH), lambda b: (s0 + b, 0))
    out = pl.BlockSpec((seq, CONV_WIDTH), lambda b: (b, 0))
    vec = _fixed_spec((1, CONV_WIDTH), l)
    pad_buf = pltpu.VMEM((seq + 2 * PAD, CONV_WIDTH), F32)
    return pl.pallas_call(
        functools.partial(_local_kernel, seq=seq),
        grid=(n_seq,),
        in_specs=[blk, blk, _fixed_spec((CONV_KERNEL, CONV_WIDTH), l), vec, vec, vec],
        out_specs=[out, out],
        out_shape=[jax.ShapeDtypeStruct((n_seq * seq, CONV_WIDTH), BF16),
                   jax.ShapeDtypeStruct((n_seq * seq, POOL_WIDTH), BF16)],
        scratch_shapes=[pad_buf] * 5 + [
            pltpu.VMEM((SUBLANES, LOCAL_CHUNK + CONV_HALO, CONV_WIDTH), F32)],
        compiler_params=_params(1),
        name="local_mixers",
    )(u, pu, p["conv_dw_w"], p["conv_dw_b"], p["conv_ln_g"], p["conv_ln_b"])


def _merge_kernel(x_ref, mod_ref, oc_ref, od_ref, cuc_ref, cud_ref, pdc_ref, pdd_ref,
                  win_hbm, bg_ref, wa_hbm, wc_hbm, wgrp_ref, ps_ref, wp_hbm, wo_hbm, bo_ref,
                  g_ref, b_ref, out_ref, merged_ref,
                  wg_ref, wg_stage, wg_sem, wa_ref, wc_ref, wp_ref, wo_ref, sq_stage, sq_sem, *, l):
    @pl.when(pl.program_id(0) == 0)
    def _():
        _stream_cast(lambda r0, n: win_hbm.at[l, pl.ds(r0, n), pl.ds(PROJ_COLS, GATE_COLS)],
                     wg_ref, wg_stage, wg_sem)
        for hbm, ref in ((wa_hbm, wa_ref), (wc_hbm, wc_ref), (wp_hbm, wp_ref), (wo_hbm, wo_ref)):
            _stream_cast(lambda r0, n, hbm=hbm: hbm.at[l, pl.ds(r0, n), :], ref, sq_stage, sq_sem)

    is_ctx = pl.program_id(0) < N_CTX_TILES
    x = x_ref[...]
    h = (x * (1.0 + mod_ref[0, 4:5, :]) + mod_ref[0, 3:4, :]).astype(BF16)
    att = jnp.where(is_ctx, oc_ref[...], od_ref[...])
    cu = jnp.where(is_ctx, cuc_ref[...], cud_ref[...])
    pd = jnp.where(is_ctx, pdc_ref[...], pdd_ref[...])
    pooled = (_dot(pd, wgrp_ref[...]) * ps_ref[...]).astype(BF16)
    cw = 256
    for j in range(D_MODEL // cw):
        sl = slice(j * cw, (j + 1) * cw)
        branches = (_dot(att, wa_ref[:, sl]), _dot(cu, wc_ref[:, sl]), _dot(pooled, wp_ref[:, sl]))
        merged = None
        for n, br in enumerate(branches):
            gs = slice(n * D_MODEL + j * cw, n * D_MODEL + (j + 1) * cw)
            gate = jax.nn.sigmoid(_dot(h, wg_ref[:, gs]) + bg_ref[:, gs])
            merged = gate * br if merged is None else merged + gate * br
        merged_ref[:, sl] = merged.astype(BF16)
    mix = _dot(merged_ref[...], wo_ref[...]) + bo_ref[...]
    out_ref[...] = _layer_norm(ALPHA * x + mod_ref[0, 5:6, :] * mix, g_ref[...], b_ref[...])


def _merge(x, o, cu, pd, p, l):
    row = lambda w: pl.BlockSpec((TM, w), lambda i: (i, 0))
    return pl.pallas_call(
        functools.partial(_merge_kernel, l=l),
        grid=(N_TILES,),
        in_specs=[
            row(D_MODEL),
            _mod_spec(l),
            *_group_specs(ATT_WIDTH), *_group_specs(CONV_WIDTH), *_group_specs(POOL_WIDTH),
            _HBM, _fixed_spec((1, GATE_COLS), l),
            _HBM, _HBM,
            _fixed_spec((POOL_WIDTH, POOL_WIDTH), l), _fixed_spec((1, POOL_WIDTH), l),
            _HBM,
            _HBM, _fixed_spec((1, D_MODEL), l),
            _fixed_spec((1, D_MODEL), l, 1), _fixed_spec((1, D_MODEL), l, 1),
        ],
        out_specs=row(D_MODEL),
        out_shape=jax.ShapeDtypeStruct((N_TOK, D_MODEL), F32),
        scratch_shapes=[pltpu.VMEM((TM, D_MODEL), BF16)]
        + _weight_scratch(D_MODEL, GATE_COLS, STAGE_ROWS_WIDE)
        + [pltpu.VMEM((ATT_WIDTH, D_MODEL), BF16), pltpu.VMEM((CONV_WIDTH, D_MODEL), BF16),
           pltpu.VMEM((POOL_WIDTH, D_MODEL), BF16)]
        + _weight_scratch(D_MODEL, D_MODEL, STAGE_ROWS),
        compiler_params=_params(1),
        name="merge",
    )(x, p["mod"], *o, *cu, *pd, p["w_in"], p["b_gate"], p["w_att_o"], p["w_conv_o"],
      p["w_pool_g"], p["pool_scale"], p["w_pool_o"], p["w_out"], p["b_out"], p["ln_g"], p["ln_b"])


def _block_diag(w_grp):
    eye = jnp.eye(POOL_GROUPS, dtype=w_grp.dtype)
    return jnp.einsum("lgcd,gh->lgchd", w_grp, eye).reshape(-1, POOL_WIDTH, POOL_WIDTH)


def kernel(x_prompt, x_sample, cache_k, cache_v, c, c_ctx, w_mod, b_mod, w_ffn_in, w_ffn_out, ln_g, ln_b, w_in, b_in, lambda_qk, subln_g, w_att_o, conv_dw_w, conv_dw_b, conv_ln_g, conv_ln_b, w_conv_o, w_pool_g, pool_scale, w_pool_o, w_out, b_out):
    x = (x_prompt.reshape(N_CTX_TOK, D_MODEL), x_sample.reshape(N_DEC_TOK, D_MODEL))
    cond = jnp.zeros((COND_ROWS, D_MODEL), F32).at[0].set(c_ctx).at[1:1 + DEC_BATCH].set(c)
    rows = lambda a: a.reshape(a.shape[:-1] + (1, a.shape[-1]))
    p = dict(
        mod=_modulation(cond, w_mod, b_mod).reshape(DEPTH, COND_ROWS, N_MOD, D_MODEL),
        rope=_rope_tables(),
        past_k=cache_k.reshape(DEC_BATCH, DEPTH, PAST_LEN, QK_COLS),
        past_v=cache_v.reshape(DEC_BATCH, DEPTH, PAST_LEN, ATT_WIDTH),
        w_ffn_in=w_ffn_in, w_ffn_out=w_ffn_out, ln_g=rows(ln_g), ln_b=rows(ln_b),
        w_in=w_in, b_in=rows(b_in), b_gate=rows(b_in[:, PROJ_COLS:]),
        lambda_qk=lambda_qk, subln_g=rows(subln_g),
        conv_dw_w=conv_dw_w, conv_dw_b=rows(conv_dw_b),
        conv_ln_g=rows(conv_ln_g), conv_ln_b=rows(conv_ln_b),
        w_att_o=w_att_o, w_conv_o=w_conv_o,
        w_pool_g=_block_diag(w_pool_g).astype(BF16), pool_scale=rows(pool_scale),
        w_pool_o=w_pool_o, w_out=w_out, b_out=rows(b_out),
    )

    new_k, new_v = [], []
    for l in range(DEPTH):
        x = _ffn(x, p, l, sub=0)
        q, k, v, u, pu, kf, vf = _inproj(x, p, l)
        new_k.append(kf.reshape(BATCH, SEQ, ATT_HEADS, 2, QK_HEAD_DIM))
        new_v.append(vf.reshape(BATCH, SEQ, ATT_HEADS, V_HEAD_DIM))
        o = (_attention_ctx(q, k, v, p, l), _attention_dec(q, k, v, p, l))
        cu_ctx, pd_ctx = _local(u, pu, p, l, n_seq=BATCH, seq=SEQ, row0=0)
        cu_dec, pd_dec = _local(u, pu, p, l, n_seq=DEC_BATCH, seq=DEC_SEQ, row0=N_CTX_TOK)
        x = _merge(x, o, (cu_ctx, cu_dec), (pd_ctx, pd_dec), p, l)
        x = _ffn(x, p, l, sub=2, split_out=(l == DEPTH - 1))

    y_prompt = x[0].reshape(BATCH, SEQ, D_MODEL)
    y_sample = x[1].reshape(DEC_BATCH, DEC_SEQ, D_MODEL)
    return (y_prompt, y_sample, jnp.stack(new_k, axis=1), jnp.stack(new_v, axis=1))
```

```python
import functools
import math

import numpy as np
import jax
import jax.numpy as jnp
from jax import lax
from jax.experimental import pallas as pl
from jax.experimental.pallas import tpu as pltpu

F32 = jnp.float32
BF16 = jnp.bfloat16

D_MODEL = 1024
BATCH = 32
SEQ = 256
DEPTH = 4
DEC_BATCH = 8
DEC_SEQ = 2048
PAST_LEN = 512
GRID_W = 64
ATT_HEADS = 4
QK_HEAD_DIM = 64
V_HEAD_DIM = 2 * QK_HEAD_DIM
ATT_WIDTH = ATT_HEADS * V_HEAD_DIM
QK_COLS = ATT_HEADS * 2 * QK_HEAD_DIM
ROPE_BASE = 10000.0
CONV_WIDTH = D_MODEL // 4
CONV_KERNEL = 31
POOL_WIDTH = D_MODEL // 4
POOL_WINDOWS = (2, 4, 8, 16)
POOL_GROUPS = 4
POOL_GROUP_DIM = POOL_WIDTH // POOL_GROUPS
N_BRANCH = 3
D_FF = ((8 * D_MODEL // 3 + 127) // 128) * 128
N_MOD = 9
ALPHA = (2 * DEPTH) ** 0.25
LN_EPS = 1e-5

PROJ_COLS = 2 * QK_COLS + ATT_WIDTH + 2 * CONV_WIDTH + POOL_WIDTH
GATE_COLS = N_BRANCH * D_MODEL

N_CTX_TOK = BATCH * SEQ
N_DEC_TOK = DEC_BATCH * DEC_SEQ
N_TOK = N_CTX_TOK + N_DEC_TOK

SUBLANES = 8
LANES = 128
TM = 1024
N_CTX_TILES = N_CTX_TOK // TM
DEC_TILES_PER_SEQ = DEC_SEQ // TM
N_TILES = N_TOK // TM
N_SUB_TILES = 4
_SUB_TILES = [slice(s * (TM // N_SUB_TILES), (s + 1) * (TM // N_SUB_TILES))
              for s in range(N_SUB_TILES)]
COND_ROWS = 16
FF_CHUNK = 256
MOD_TN = 1536
ATT_ROWS = 256
KEY_CHUNK = 256
ROW_BLOCK = 64
DEC_TQ = 8 * ATT_ROWS
CTX_SEQS_PER_STEP = 2
PAD = 16
LOCAL_CHUNK = 128
CONV_SHIFT_GROUPS = -(-(CONV_KERNEL + 1) // SUBLANES)
CONV_HALO = (CONV_SHIFT_GROUPS - 1) * SUBLANES

Q_SCALE = QK_HEAD_DIM ** -0.5 * math.log2(math.e)

VMEM_LIMIT = 56 * 1024 * 1024


def _cond_row(i):
    return jnp.where(i < N_CTX_TILES, 0, 1 + (i - N_CTX_TILES) // DEC_TILES_PER_SEQ)


def _ctx_tile(i):
    return (jnp.minimum(i, N_CTX_TILES - 1), 0)


def _dec_tile(i):
    return (jnp.maximum(i - N_CTX_TILES, 0), 0)


def _group_specs(width):
    return [pl.BlockSpec((TM, width), _ctx_tile), pl.BlockSpec((TM, width), _dec_tile)]


def _layer_norm(r, g, b):
    mu = jnp.mean(r, axis=-1, keepdims=True)
    d = r - mu
    var = jnp.mean(d * d, axis=-1, keepdims=True)
    return d * lax.rsqrt(var + LN_EPS) * g + b


def _dot(a, b):
    return jnp.dot(a, b, preferred_element_type=F32)


def _fixed_spec(tail, *lead):
    index = tuple(lead) + (0,) * len(tail)
    return pl.BlockSpec((None,) * len(lead) + tuple(tail), lambda *_: index,
                        pipeline_mode=pl.Buffered(1))


def _mod_spec(l):
    return pl.BlockSpec((None, 1, N_MOD, D_MODEL), lambda i: (l, _cond_row(i), 0, 0))


def _params(n_grid_dims):
    return pltpu.CompilerParams(dimension_semantics=("arbitrary",) * n_grid_dims,
                                vmem_limit_bytes=VMEM_LIMIT)


def _mod_kernel(cond_ref, w_ref, b_ref, o_ref):
    a = jax.nn.silu(cond_ref[...]).astype(BF16)
    o_ref[0] = _dot(a, w_ref[0].astype(BF16)) + b_ref[0]


def _modulation(cond, w_mod, b_mod):
    n_col = N_MOD * D_MODEL
    return pl.pallas_call(
        _mod_kernel,
        grid=(DEPTH, n_col // MOD_TN),
        in_specs=[
            pl.BlockSpec((COND_ROWS, D_MODEL), lambda l, j: (0, 0)),
            pl.BlockSpec((1, D_MODEL, MOD_TN), lambda l, j: (l, 0, j)),
            pl.BlockSpec((1, 1, MOD_TN), lambda l, j: (l, 0, j)),
        ],
        out_specs=pl.BlockSpec((1, COND_ROWS, MOD_TN), lambda l, j: (l, 0, j)),
        out_shape=jax.ShapeDtypeStruct((DEPTH, COND_ROWS, n_col), F32),
        compiler_params=_params(2),
        name="modulation",
    )(cond, w_mod, b_mod.reshape(DEPTH, 1, n_col))


def _ffn_kernel(*refs, sub, split_in, split_out):
    n_x = 2 if split_in else 1
    n_o = 2 if split_out else 1
    x_refs = refs[:n_x]
    mod_ref, win_ref, wout_ref, g_ref, b_ref = refs[n_x:n_x + 5]
    o_refs = refs[n_x + 5:n_x + 5 + n_o]
    act_ref = refs[n_x + 5 + n_o]
    is_ctx = pl.program_id(0) < N_CTX_TILES
    shift = mod_ref[0, 3 * sub:3 * sub + 1, :]
    scale = mod_ref[0, 3 * sub + 1:3 * sub + 2, :]
    gate = mod_ref[0, 3 * sub + 2:3 * sub + 3, :]

    def load_x(rows):
        if split_in:
            return jnp.where(is_ctx, x_refs[0][rows, :], x_refs[1][rows, :])
        return x_refs[0][rows, :]

    for s, rows in enumerate(_SUB_TILES):
        h = (load_x(rows) * (1.0 + scale) + shift).astype(BF16)
        for j in range(D_FF // FF_CHUNK):
            lo = j * FF_CHUNK
            g = _dot(h, win_ref[:, lo:lo + FF_CHUNK])
            u = _dot(h, win_ref[:, D_FF + lo:D_FF + lo + FF_CHUNK])
            act_ref[s, :, lo:lo + FF_CHUNK] = (jax.nn.silu(g) * u).astype(BF16)
    for s, rows in enumerate(_SUB_TILES):
        y = _dot(act_ref[s], wout_ref[...])
        out = _layer_norm(ALPHA * load_x(rows) + 0.5 * gate * y, g_ref[...], b_ref[...])
        if split_out:
            @pl.when(is_ctx)
            def _():
                o_refs[0][rows, :] = out

            @pl.when(jnp.logical_not(is_ctx))
            def _():
                o_refs[1][rows, :] = out
        else:
            o_refs[0][rows, :] = out


def _ffn(x, p, l, sub, split_out=False):
    split_in = isinstance(x, tuple)
    xs = x if split_in else (x,)
    row = pl.BlockSpec((TM, D_MODEL), lambda i: (i, 0))
    if split_out:
        out_specs = _group_specs(D_MODEL)
        out_shape = [jax.ShapeDtypeStruct((N_CTX_TOK, D_MODEL), F32),
                     jax.ShapeDtypeStruct((N_DEC_TOK, D_MODEL), F32)]
    else:
        out_specs = row
        out_shape = jax.ShapeDtypeStruct((N_TOK, D_MODEL), F32)
    return pl.pallas_call(
        functools.partial(_ffn_kernel, sub=sub, split_in=split_in, split_out=split_out),
        grid=(N_TILES,),
        in_specs=(_group_specs(D_MODEL) if split_in else [row]) + [
            _mod_spec(l),
            _fixed_spec((D_MODEL, 2 * D_FF), l, sub // 2),
            _fixed_spec((D_FF, D_MODEL), l, sub // 2),
            _fixed_spec((1, D_MODEL), l, sub),
            _fixed_spec((1, D_MODEL), l, sub),
        ],
        out_specs=out_specs,
        out_shape=out_shape,
        scratch_shapes=[pltpu.VMEM((N_SUB_TILES, TM // N_SUB_TILES, D_FF), BF16)],
        compiler_params=_params(1),
        name="ffn",
    )(*xs, p["mod"], p["w_ffn_in"], p["w_ffn_out"], p["ln_g"], p["ln_b"])


def _rope_tables():
    nf = QK_HEAD_DIM // 4
    t = np.arange(DEC_SEQ)
    lane = np.arange(V_HEAD_DIM)
    d = lane % QK_HEAD_DIM
    by_col = d >= QK_HEAD_DIM // 2
    e = d % (QK_HEAD_DIM // 2)
    inv = ROPE_BASE ** (-(e % nf).astype(np.float64) / nf)
    pos = np.where(by_col[None, :], (t % GRID_W)[:, None], (t // GRID_W)[:, None])
    ang = pos.astype(np.float64) * inv[None, :]
    first = (e < nf)[None, :]
    cos = np.cos(ang)
    sin_next = np.where(first, -np.sin(ang), 0.0)
    sin_prev = np.where(first, 0.0, np.sin(ang))
    ident = np.zeros((TM, V_HEAD_DIM))
    cos = np.concatenate([cos, ident + 1.0], axis=0)
    sin_next = np.concatenate([sin_next, ident], axis=0)
    sin_prev = np.concatenate([sin_prev, ident], axis=0)
    return (jnp.asarray(cos, F32), jnp.asarray(sin_next, F32), jnp.asarray(sin_prev, F32))


def _inproj_kernel(x_ref, mod_ref, w_ref, b_ref, cos_ref, sn_ref, sp_ref,
                   q_ref, k_ref, v_ref, u_ref, pu_ref, kf_ref, vf_ref):
    is_ctx = pl.program_id(0) < N_CTX_TILES
    x = x_ref[...]
    h = (x * (1.0 + mod_ref[0, 4:5, :]) + mod_ref[0, 3:4, :]).astype(BF16)
    cos = cos_ref[...]
    sn = sn_ref[...]
    sp = sp_ref[...]
    half = QK_HEAD_DIM // 4

    def proj(lo, width):
        return _dot(h, w_ref[:, lo:lo + width]) + b_ref[:, lo:lo + width]

    def rope(z):
        up = pltpu.roll(z, V_HEAD_DIM - half, axis=1)
        dn = pltpu.roll(z, half, axis=1)
        return z * cos + up * sn + dn * sp

    zq = proj(0, QK_COLS)
    zk = proj(QK_COLS, QK_COLS)
    for hd in range(ATT_HEADS):
        sl = slice(hd * V_HEAD_DIM, (hd + 1) * V_HEAD_DIM)
        q_ref[:, sl] = (rope(zq[:, sl]) * Q_SCALE).astype(BF16)
        k_ref[:, sl] = rope(zk[:, sl]).astype(BF16)
    zv = proj(2 * QK_COLS, ATT_WIDTH)
    v_ref[...] = zv.astype(BF16)

    @pl.when(is_ctx)
    def _():
        kf_ref[...] = zk
        vf_ref[...] = zv

    c1 = 2 * QK_COLS + ATT_WIDTH
    zc = proj(c1, 2 * CONV_WIDTH)
    u_ref[...] = zc[:, :CONV_WIDTH] * jax.nn.sigmoid(zc[:, CONV_WIDTH:])
    pu_ref[...] = proj(c1 + 2 * CONV_WIDTH, POOL_WIDTH)


def _inproj(x, p, l):
    cos, sn, sp = p["rope"]

    def tab_map(i):
        return (jnp.where(i < N_CTX_TILES, DEC_TILES_PER_SEQ,
                          (i - N_CTX_TILES) % DEC_TILES_PER_SEQ), 0)

    tab_spec = pl.BlockSpec((TM, V_HEAD_DIM), tab_map)
    row = lambda w: pl.BlockSpec((TM, w), lambda i: (i, 0))
    return pl.pallas_call(
        _inproj_kernel,
        grid=(N_TILES,),
        in_specs=[
            row(D_MODEL),
            _mod_spec(l),
            _fixed_spec((D_MODEL, PROJ_COLS), l),
            _fixed_spec((1, PROJ_COLS), l),
            tab_spec, tab_spec, tab_spec,
        ],
        out_specs=[
            row(QK_COLS), row(QK_COLS), row(ATT_WIDTH), row(CONV_WIDTH), row(POOL_WIDTH),
            pl.BlockSpec((TM, QK_COLS), _ctx_tile),
            pl.BlockSpec((TM, ATT_WIDTH), _ctx_tile),
        ],
        out_shape=[
            jax.ShapeDtypeStruct((N_TOK, QK_COLS), BF16),
            jax.ShapeDtypeStruct((N_TOK, QK_COLS), BF16),
            jax.ShapeDtypeStruct((N_TOK, ATT_WIDTH), BF16),
            jax.ShapeDtypeStruct((N_TOK, CONV_WIDTH), F32),
            jax.ShapeDtypeStruct((N_TOK, POOL_WIDTH), F32),
            jax.ShapeDtypeStruct((N_CTX_TOK, QK_COLS), F32),
            jax.ShapeDtypeStruct((N_CTX_TOK, ATT_WIDTH), F32),
        ],
        compiler_params=_params(1),
        name="inproj",
    )(x, p["mod"], p["w_proj"], p["b_in"], cos, sn, sp)


def _interleave(*stages):
    for i in range(max(len(st) for st in stages)):
        for st in stages:
            if i < len(st):
                st[i]()


def _attention_units(units, lam, lam_init, bufs, n_keys, group):
    n_chunks = n_keys // KEY_CHUNK
    n_tiles = n_keys // LANES
    n_blocks = 2 * ATT_ROWS // ROW_BLOCK
    nt = (((1,), (1,)), ((), ()))
    state = [dict() for _ in units]
    n_groups = len(units) // group

    def buf(n):
        return bufs[(n // group) % 2][n % group]

    def q_stage(n):
        s_ref = buf(n)[0]
        st = state[n]

        def first():
            q = units[n][0]()
            lane = lax.broadcasted_iota(jnp.int32, q.shape, 1)
            zero = jnp.zeros_like(q)
            st["qq"] = jnp.concatenate([jnp.where(lane < QK_HEAD_DIM, q, zero),
                                        jnp.where(lane >= QK_HEAD_DIM, q, zero)], axis=0)

        def chunk(c):
            cols = slice(c * KEY_CHUNK, (c + 1) * KEY_CHUNK)
            s_ref[:, cols] = lax.dot_general(st["qq"], units[n][1](c), nt,
                                             preferred_element_type=F32)
        return [first] + [functools.partial(chunk, c) for c in range(n_chunks)]

    def e_stage(n):
        s_ref, p_ref = buf(n)
        dens = state[n]["dens"] = [None] * n_blocks

        def block(r):
            rows = slice(r * ROW_BLOCK, (r + 1) * ROW_BLOCK)
            tile = lambda t: s_ref[rows, t * LANES:(t + 1) * LANES]
            m = functools.reduce(jnp.maximum, [tile(t) for t in range(n_tiles)])
            m = jnp.broadcast_to(jnp.max(m, axis=-1, keepdims=True), (ROW_BLOCK, LANES))
            d = None
            for t in range(n_tiles):
                e = jnp.exp2(tile(t) - m)
                d = e if d is None else d + e
                p_ref[rows, t * LANES:(t + 1) * LANES] = e.astype(BF16)
            dens[r] = jnp.sum(d, axis=-1, keepdims=True)
        return [functools.partial(block, r) for r in range(n_blocks)]

    def v_stage(n):
        p_ref = buf(n)[1]
        st = state[n]

        def first():
            dens = st["dens"]
            st["d1"] = jnp.concatenate(dens[:n_blocks // 2], axis=0)
            d2 = jnp.concatenate(dens[n_blocks // 2:], axis=0)
            st["c"] = (lam * st["d1"] / d2).astype(BF16)
            st["o"] = jnp.zeros((ATT_ROWS, V_HEAD_DIM), F32)

        def chunk(c):
            cols = slice(c * KEY_CHUNK, (c + 1) * KEY_CHUNK)
            w = p_ref[0:ATT_ROWS, cols] - st["c"] * p_ref[ATT_ROWS:2 * ATT_ROWS, cols]
            st["o"] = st["o"] + _dot(w, units[n][2](c))

        def last():
            o = st["o"] * (1.0 / st["d1"])
            y = o * lax.rsqrt(jnp.mean(o * o, axis=-1, keepdims=True) + LN_EPS)
            units[n][4]((y * units[n][3]()) * (1.0 - lam_init))
        return [first] + [functools.partial(chunk, c) for c in range(n_chunks)] + [last]

    for t in range(n_groups + 2):
        stages = []
        for stage, g in ((q_stage, t), (e_stage, t - 1), (v_stage, t - 2)):
            if 0 <= g < n_groups:
                stages += [stage(n) for n in range(g * group, (g + 1) * group)]
        _interleave(*stages)


def _attention_units_keys_on_rows(units, lam, lam_init, bufs, n_keys, group):
    n_chunks = n_keys // KEY_CHUNK
    width = 2 * ATT_ROWS
    nt = (((1,), (1,)), ((), ()))
    tn = (((0,), (0,)), ((), ()))
    state = [dict() for _ in units]
    n_groups = len(units) // group

    def fold(x, op):
        return functools.reduce(op, [x[r:r + SUBLANES, :] for r in range(0, KEY_CHUNK, SUBLANES)])

    def buf(n):
        return bufs[(n // group) % 2][n % group]

    def q_stage(n):
        s_ref = buf(n)[0]
        st = state[n]

        def first():
            q = units[n][0]()
            lane = lax.broadcasted_iota(jnp.int32, q.shape, 1)
            zero = jnp.zeros_like(q)
            st["qq"] = jnp.concatenate([jnp.where(lane < QK_HEAD_DIM, q, zero),
                                        jnp.where(lane >= QK_HEAD_DIM, q, zero)], axis=0)
            st["m"] = None

        def chunk(c):
            s = lax.dot_general(units[n][1](c), st["qq"], nt, preferred_element_type=F32)
            s_ref[c * KEY_CHUNK:(c + 1) * KEY_CHUNK, :] = s
            m = fold(s, jnp.maximum)
            st["m"] = m if st["m"] is None else jnp.maximum(st["m"], m)
        return [first] + [functools.partial(chunk, c) for c in range(n_chunks)]

    def e_stage(n):
        s_ref, p_ref = buf(n)
        st = state[n]

        def first():
            st["m_row"] = jnp.max(st["m"], axis=0, keepdims=True)
            st["d"] = None

        def chunk(c):
            rows = slice(c * KEY_CHUNK, (c + 1) * KEY_CHUNK)
            e = jnp.exp2(s_ref[rows, :] - st["m_row"])
            d = fold(e, jnp.add)
            st["d"] = d if st["d"] is None else st["d"] + d
            p_ref[rows, :] = e.astype(BF16)
        return [first] + [functools.partial(chunk, c) for c in range(n_chunks)]

    def v_stage(n):
        p_ref = buf(n)[1]
        st = state[n]

        def first():
            den = jnp.sum(st["d"], axis=0, keepdims=True)
            st["a"] = (1.0 / den[:, :ATT_ROWS]).astype(BF16)
            st["b"] = (lam / den[:, ATT_ROWS:]).astype(BF16)
            st["o"] = jnp.zeros((ATT_ROWS, V_HEAD_DIM), F32)

        def chunk(c):
            rows = slice(c * KEY_CHUNK, (c + 1) * KEY_CHUNK)
            w = p_ref[rows, 0:ATT_ROWS] * st["a"] - p_ref[rows, ATT_ROWS:width] * st["b"]
            st["o"] = st["o"] + lax.dot_general(w, units[n][2](c), tn,
                                                preferred_element_type=F32)

        def last():
            o = st["o"]
            y = o * lax.rsqrt(jnp.mean(o * o, axis=-1, keepdims=True) + LN_EPS)
            units[n][4]((y * units[n][3]()) * (1.0 - lam_init))
        return [first] + [functools.partial(chunk, c) for c in range(n_chunks)] + [last]

    for t in range(n_groups + 2):
        stages = []
        for stage, g in ((q_stage, t), (e_stage, t - 1), (v_stage, t - 2)):
            if 0 <= g < n_groups:
                stages += [stage(n) for n in range(g * group, (g + 1) * group)]
        _interleave(*stages)


def _lam_init(l):
    return 0.8 - 0.6 * math.exp(-0.3 * l)


def _lambda(lq_ref, lam_init):
    lq = lq_ref[...]
    return (jnp.exp(jnp.sum(lq[0:1] * lq[1:2], axis=-1, keepdims=True))
            - jnp.exp(jnp.sum(lq[2:3] * lq[3:4], axis=-1, keepdims=True)) + lam_init)


def _attn_ctx_kernel(lq_ref, g_ref, q_ref, k_ref, v_ref, o_ref, s_all, p_all, *, lam_init):
    def store(rows, cols, y):
        o_ref[rows, cols] = y.astype(o_ref.dtype)

    units = []
    for sq in range(CTX_SEQS_PER_STEP):
        rows = slice(sq * SEQ, (sq + 1) * SEQ)
        for hd in range(ATT_HEADS):
            cols = slice(hd * V_HEAD_DIM, (hd + 1) * V_HEAD_DIM)
            units.append((lambda rows=rows, cols=cols: q_ref[rows, cols],
                          lambda c, rows=rows, cols=cols: k_ref[rows, cols],
                          lambda c, rows=rows, cols=cols: v_ref[rows, cols],
                          lambda hd=hd: g_ref[hd],
                          functools.partial(store, rows, cols)))
    bufs = [[(s_all.at[par * ATT_HEADS + i], p_all.at[par * ATT_HEADS + i])
             for i in range(ATT_HEADS)] for par in range(2)]
    _attention_units_keys_on_rows(units, _lambda(lq_ref, lam_init), lam_init, bufs, SEQ,
                                  group=ATT_HEADS)


def _attention_ctx(q, k, v, p, l):
    rows = CTX_SEQS_PER_STEP * SEQ
    blk = pl.BlockSpec((rows, ATT_WIDTH), lambda b: (b, 0))
    s_buf = pltpu.VMEM((2 * ATT_HEADS, SEQ, 2 * ATT_ROWS), F32)
    p_buf = pltpu.VMEM((2 * ATT_HEADS, SEQ, 2 * ATT_ROWS), BF16)
    return pl.pallas_call(
        functools.partial(_attn_ctx_kernel, lam_init=_lam_init(l)),
        grid=(BATCH // CTX_SEQS_PER_STEP,),
        in_specs=[
            _fixed_spec((4, QK_HEAD_DIM), l),
            _fixed_spec((ATT_HEADS, 1, V_HEAD_DIM), l),
            blk, blk, blk,
        ],
        out_specs=blk,
        out_shape=jax.ShapeDtypeStruct((N_CTX_TOK, ATT_WIDTH), BF16),
        scratch_shapes=[s_buf, p_buf],
        compiler_params=_params(1),
        name="attention_ctx",
    )(p["lambda_qk"], p["subln_g"], q, k, v)


def _attn_dec_kernel(lq_ref, g_ref, q_ref, pk_ref, pv_ref, k_ref, v_ref, o_ref,
                     kcat, vcat, s0, s1, p0, p1, *, lam_init):
    @pl.when(pl.program_id(2) == 0)
    def _():
        kcat[0:PAST_LEN, :] = pk_ref[...].astype(BF16)
        kcat[PAST_LEN:PAST_LEN + DEC_SEQ, :] = k_ref[...]
        vcat[0:PAST_LEN, :] = pv_ref[...].astype(BF16)
        vcat[PAST_LEN:PAST_LEN + DEC_SEQ, :] = v_ref[...]

    def store(rows, y):
        o_ref[rows, :] = y.astype(o_ref.dtype)

    def chunk_of(ref):
        return lambda c: ref[c * KEY_CHUNK:(c + 1) * KEY_CHUNK, :]

    units = []
    for r0 in range(0, DEC_TQ, ATT_ROWS):
        rows = slice(r0, r0 + ATT_ROWS)
        units.append((lambda rows=rows: q_ref[rows, :], chunk_of(kcat), chunk_of(vcat),
                      lambda: g_ref[0], functools.partial(store, rows)))
    _attention_units(units, _lambda(lq_ref, lam_init), lam_init, [[(s0, p0)], [(s1, p1)]],
                     PAST_LEN + DEC_SEQ, group=1)


def _attention_dec(q, k, v, p, l):
    nq = DEC_SEQ // DEC_TQ
    q0 = N_CTX_TOK // DEC_TQ
    s0 = N_CTX_TOK // DEC_SEQ
    n_keys = PAST_LEN + DEC_SEQ
    past = pl.BlockSpec((None, None, PAST_LEN, V_HEAD_DIM), lambda b, h, j: (b, l, 0, h))
    own = pl.BlockSpec((DEC_SEQ, V_HEAD_DIM), lambda b, h, j: (s0 + b, h))
    kv_buf = pltpu.VMEM((n_keys, V_HEAD_DIM), BF16)
    s_buf = pltpu.VMEM((2 * ATT_ROWS, n_keys), F32)
    p_buf = pltpu.VMEM((2 * ATT_ROWS, n_keys), BF16)
    return pl.pallas_call(
        functools.partial(_attn_dec_kernel, lam_init=_lam_init(l)),
        grid=(DEC_BATCH, ATT_HEADS, nq),
        in_specs=[
            _fixed_spec((4, QK_HEAD_DIM), l),
            pl.BlockSpec((None, 1, 1, V_HEAD_DIM), lambda b, h, j: (l, h, 0, 0)),
            pl.BlockSpec((DEC_TQ, V_HEAD_DIM), lambda b, h, j: (q0 + b * nq + j, h)),
            past, past, own, own,
        ],
        out_specs=pl.BlockSpec((DEC_TQ, V_HEAD_DIM), lambda b, h, j: (b * nq + j, h)),
        out_shape=jax.ShapeDtypeStruct((N_DEC_TOK, ATT_WIDTH), BF16),
        scratch_shapes=[kv_buf, kv_buf, s_buf, s_buf, p_buf, p_buf],
        compiler_params=_params(3),
        name="attention_dec",
    )(p["lambda_qk"], p["subln_g"], q, p["past_k"], p["past_v"], k, v)


def _local_kernel(u_ref, pu_ref, cw_ref, cb_ref, g_ref, b_ref, cu_ref, pd_ref,
                  upad, ppad, s2, s4, s8, shifted, *, seq):
    zpad = jnp.zeros((PAD, CONV_WIDTH), F32)
    for buf in (upad, ppad, s2, s4, s8):
        buf[0:PAD, :] = zpad
        buf[PAD + seq:PAD + seq + PAD, :] = zpad
    upad[PAD:PAD + seq, :] = u_ref[...]
    ppad[PAD:PAD + seq, :] = pu_ref[...]

    n_chunks = seq // LOCAL_CHUNK
    r = LOCAL_CHUNK
    centre = CONV_KERNEL // 2
    n_shift_groups = CONV_SHIFT_GROUPS
    halo = CONV_HALO

    for c in range(n_chunks):
        win0 = c * r
        for shift in range(1, SUBLANES):
            shifted[shift] = upad[win0 + shift:win0 + shift + r + halo, :]
        acc = jnp.zeros((r, CONV_WIDTH), F32)
        for shift in range(SUBLANES):
            for a in range(n_shift_groups):
                t = SUBLANES * a + shift - (PAD - centre)
                if 0 <= t < CONV_KERNEL:
                    lo = SUBLANES * a
                    window = (shifted[shift, lo:lo + r, :] if shift
                              else upad[win0 + lo:win0 + lo + r, :])
                    acc = acc + window * cw_ref[t:t + 1, :]
        y = _layer_norm(acc + cb_ref[...], g_ref[...], b_ref[...])
        cu_ref[c * r:(c + 1) * r, :] = jax.nn.silu(y).astype(cu_ref.dtype)

    ext = seq + PAD
    s2[8:8 + ext, :] = ppad[7:7 + ext, :] + ppad[8:8 + ext, :]
    s4[8:8 + ext, :] = s2[7:7 + ext, :] + s2[9:9 + ext, :]
    s8[8:8 + ext, :] = s4[6:6 + ext, :] + s4[10:10 + ext, :]
    lane = lax.broadcasted_iota(jnp.int32, (r, POOL_WIDTH), 1)
    grp = lane // POOL_GROUP_DIM
    half_win = jnp.where(grp == 0, 1, jnp.where(grp == 1, 2, jnp.where(grp == 2, 4, 8)))
    for c in range(n_chunks):
        base = PAD + c * r
        sl = slice(base, base + r)
        s16 = s8[base - 4:base - 4 + r, :] + s8[base + 4:base + 4 + r, :]
        tot = jnp.where(grp == 0, s2[sl, :],
                        jnp.where(grp == 1, s4[sl, :], jnp.where(grp == 2, s8[sl, :], s16)))
        t = lax.broadcasted_iota(jnp.int32, (r, POOL_WIDTH), 0) + c * r
        cnt = jnp.clip(t + half_win, 0, seq) - jnp.clip(t - half_win, 0, seq)
        pd_ref[c * r:(c + 1) * r, :] = (tot / cnt.astype(F32) - ppad[sl, :]).astype(pd_ref.dtype)


def _local(u, pu, p, l, *, n_seq, seq, row0):
    s0 = row0 // seq
    blk = pl.BlockSpec((seq, CONV_WIDTH), lambda b: (s0 + b, 0))
    out = pl.BlockSpec((seq, CONV_WIDTH), lambda b: (b, 0))
    vec = _fixed_spec((1, CONV_WIDTH), l)
    pad_buf = pltpu.VMEM((seq + 2 * PAD, CONV_WIDTH), F32)
    return pl.pallas_call(
        functools.partial(_local_kernel, seq=seq),
        grid=(n_seq,),
        in_specs=[blk, blk, _fixed_spec((CONV_KERNEL, CONV_WIDTH), l), vec, vec, vec],
        out_specs=[out, out],
        out_shape=[jax.ShapeDtypeStruct((n_seq * seq, CONV_WIDTH), BF16),
                   jax.ShapeDtypeStruct((n_seq * seq, POOL_WIDTH), BF16)],
        scratch_shapes=[pad_buf] * 5 + [
            pltpu.VMEM((SUBLANES, LOCAL_CHUNK + CONV_HALO, CONV_WIDTH), F32)],
        compiler_params=_params(1),
        name="local_mixers",
    )(u, pu, p["conv_dw_w"], p["conv_dw_b"], p["conv_ln_g"], p["conv_ln_b"])


def _merge_kernel(x_ref, mod_ref, oc_ref, od_ref, cuc_ref, cud_ref, pdc_ref, pdd_ref,
                  wg_ref, bg_ref, wa_ref, wc_ref, wgrp_ref, ps_ref, wp_ref, wo_ref, bo_ref,
                  g_ref, b_ref, out_ref, merged_ref):
    is_ctx = pl.program_id(0) < N_CTX_TILES
    cw = 256
    for rows in _SUB_TILES:
        h = (x_ref[rows, :] * (1.0 + mod_ref[0, 4:5, :]) + mod_ref[0, 3:4, :]).astype(BF16)
        att = jnp.where(is_ctx, oc_ref[rows, :], od_ref[rows, :])
        cu = jnp.where(is_ctx, cuc_ref[rows, :], cud_ref[rows, :])
        pd = jnp.where(is_ctx, pdc_ref[rows, :], pdd_ref[rows, :])
        pooled = (_dot(pd, wgrp_ref[...]) * ps_ref[...]).astype(BF16)
        for j in range(D_MODEL // cw):
            sl = slice(j * cw, (j + 1) * cw)
            branches = (_dot(att, wa_ref[:, sl]), _dot(cu, wc_ref[:, sl]),
                        _dot(pooled, wp_ref[:, sl]))
            merged = None
            for n, br in enumerate(branches):
                gs = slice(n * D_MODEL + j * cw, n * D_MODEL + (j + 1) * cw)
                gate = jax.nn.sigmoid(_dot(h, wg_ref[:, gs]) + bg_ref[:, gs])
                merged = gate * br if merged is None else merged + gate * br
            merged_ref[rows, sl] = merged.astype(BF16)
    for rows in _SUB_TILES:
        mix = _dot(merged_ref[rows, :], wo_ref[...]) + bo_ref[...]
        out_ref[rows, :] = _layer_norm(ALPHA * x_ref[rows, :] + mod_ref[0, 5:6, :] * mix,
                                       g_ref[...], b_ref[...])


def _merge(x, o, cu, pd, p, l):
    row = lambda w: pl.BlockSpec((TM, w), lambda i: (i, 0))
    return pl.pallas_call(
        _merge_kernel,
        grid=(N_TILES,),
        in_specs=[
            row(D_MODEL),
            _mod_spec(l),
            *_group_specs(ATT_WIDTH), *_group_specs(CONV_WIDTH), *_group_specs(POOL_WIDTH),
            _fixed_spec((D_MODEL, GATE_COLS), l), _fixed_spec((1, GATE_COLS), l),
            _fixed_spec((ATT_WIDTH, D_MODEL), l), _fixed_spec((CONV_WIDTH, D_MODEL), l),
            _fixed_spec((POOL_WIDTH, POOL_WIDTH), l), _fixed_spec((1, POOL_WIDTH), l),
            _fixed_spec((POOL_WIDTH, D_MODEL), l),
            _fixed_spec((D_MODEL, D_MODEL), l), _fixed_spec((1, D_MODEL), l),
            _fixed_spec((1, D_MODEL), l, 1), _fixed_spec((1, D_MODEL), l, 1),
        ],
        out_specs=row(D_MODEL),
        out_shape=jax.ShapeDtypeStruct((N_TOK, D_MODEL), F32),
        scratch_shapes=[pltpu.VMEM((TM, D_MODEL), BF16)],
        compiler_params=_params(1),
        name="merge",
    )(x, p["mod"], *o, *cu, *pd, p["w_gate"], p["b_gate"], p["w_att_o"], p["w_conv_o"],
      p["w_pool_g"], p["pool_scale"], p["w_pool_o"], p["w_out"], p["b_out"], p["ln_g"], p["ln_b"])


def _block_diag(w_grp):
    eye = jnp.eye(POOL_GROUPS, dtype=w_grp.dtype)
    return jnp.einsum("lgcd,gh->lgchd", w_grp, eye).reshape(-1, POOL_WIDTH, POOL_WIDTH)


def kernel(x_prompt, x_sample, cache_k, cache_v, c, c_ctx, w_mod, b_mod, w_ffn_in, w_ffn_out, ln_g, ln_b, w_in, b_in, lambda_qk, subln_g, w_att_o, conv_dw_w, conv_dw_b, conv_ln_g, conv_ln_b, w_conv_o, w_pool_g, pool_scale, w_pool_o, w_out, b_out):
    x = (x_prompt.reshape(N_CTX_TOK, D_MODEL), x_sample.reshape(N_DEC_TOK, D_MODEL))
    cond = jnp.zeros((COND_ROWS, D_MODEL), F32).at[0].set(c_ctx).at[1:1 + DEC_BATCH].set(c)
    rows = lambda a: a.reshape(a.shape[:-1] + (1, a.shape[-1]))
    p = dict(
        mod=_modulation(cond, w_mod, b_mod).reshape(DEPTH, COND_ROWS, N_MOD, D_MODEL),
        rope=_rope_tables(),
        past_k=cache_k.reshape(DEC_BATCH, DEPTH, PAST_LEN, QK_COLS),
        past_v=cache_v.reshape(DEC_BATCH, DEPTH, PAST_LEN, ATT_WIDTH),
        w_ffn_in=w_ffn_in.astype(BF16), w_ffn_out=w_ffn_out.astype(BF16),
        ln_g=rows(ln_g), ln_b=rows(ln_b),
        w_proj=w_in[:, :, :PROJ_COLS].astype(BF16), b_in=rows(b_in),
        w_gate=w_in[:, :, PROJ_COLS:].astype(BF16), b_gate=rows(b_in[:, PROJ_COLS:]),
        lambda_qk=lambda_qk, subln_g=rows(subln_g),
        conv_dw_w=conv_dw_w, conv_dw_b=rows(conv_dw_b),
        conv_ln_g=rows(conv_ln_g), conv_ln_b=rows(conv_ln_b),
        w_att_o=w_att_o.astype(BF16), w_conv_o=w_conv_o.astype(BF16),
        w_pool_g=_block_diag(w_pool_g).astype(BF16), pool_scale=rows(pool_scale),
        w_pool_o=w_pool_o.astype(BF16), w_out=w_out.astype(BF16), b_out=rows(b_out),
    )

    new_k, new_v = [], []
    for l in range(DEPTH):
        x = _ffn(x, p, l, sub=0)
        q, k, v, u, pu, kf, vf = _inproj(x, p, l)
        new_k.append(kf.reshape(BATCH, SEQ, ATT_HEADS, 2, QK_HEAD_DIM))
        new_v.append(vf.reshape(BATCH, SEQ, ATT_HEADS, V_HEAD_DIM))
        o = (_attention_ctx(q, k, v, p, l), _attention_dec(q, k, v, p, l))
        cu_ctx, pd_ctx = _local(u, pu, p, l, n_seq=BATCH, seq=SEQ, row0=0)
        cu_dec, pd_dec = _local(u, pu, p, l, n_seq=DEC_BATCH, seq=DEC_SEQ, row0=N_CTX_TOK)
        x = _merge(x, o, (cu_ctx, cu_dec), (pd_ctx, pd_dec), p, l)
        x = _ffn(x, p, l, sub=2, split_out=(l == DEPTH - 1))

    y_prompt = x[0].reshape(BATCH, SEQ, D_MODEL)
    y_sample = x[1].reshape(DEC_BATCH, DEC_SEQ, D_MODEL)
    return (y_prompt, y_sample, jnp.stack(new_k, axis=1), jnp.stack(new_v, axis=1))
```

```python
import functools
import math

import numpy as np
import jax
import jax.numpy as jnp
from jax import lax
from jax.experimental import pallas as pl
from jax.experimental.pallas import tpu as pltpu

F32 = jnp.float32
BF16 = jnp.bfloat16

D_MODEL = 1024
BATCH = 32
SEQ = 256
DEPTH = 4
DEC_BATCH = 8
DEC_SEQ = 2048
PAST_LEN = 512
GRID_W = 64
ATT_HEADS = 4
QK_HEAD_DIM = 64
V_HEAD_DIM = 2 * QK_HEAD_DIM
ATT_WIDTH = ATT_HEADS * V_HEAD_DIM
QK_COLS = ATT_HEADS * 2 * QK_HEAD_DIM
ROPE_BASE = 10000.0
CONV_WIDTH = D_MODEL // 4
CONV_KERNEL = 31
POOL_WIDTH = D_MODEL // 4
POOL_WINDOWS = (2, 4, 8, 16)
POOL_GROUPS = 4
POOL_GROUP_DIM = POOL_WIDTH // POOL_GROUPS
N_BRANCH = 3
D_FF = ((8 * D_MODEL // 3 + 127) // 128) * 128
N_MOD = 9
ALPHA = (2 * DEPTH) ** 0.25
LN_EPS = 1e-5

PROJ_COLS = 2 * QK_COLS + ATT_WIDTH + 2 * CONV_WIDTH + POOL_WIDTH
GATE_COLS = N_BRANCH * D_MODEL

N_CTX_TOK = BATCH * SEQ
N_DEC_TOK = DEC_BATCH * DEC_SEQ
N_TOK = N_CTX_TOK + N_DEC_TOK

SUBLANES = 8
LANES = 128
TM = 1024
N_CTX_TILES = N_CTX_TOK // TM
DEC_TILES_PER_SEQ = DEC_SEQ // TM
N_TILES = N_TOK // TM
N_SUB_TILES = 4
_SUB_TILES = [slice(s * (TM // N_SUB_TILES), (s + 1) * (TM // N_SUB_TILES))
              for s in range(N_SUB_TILES)]
COND_ROWS = 16
FF_CHUNK = 256
MOD_TN = 1536
ATT_ROWS = 256
KEY_CHUNK = 256
ROW_BLOCK = 64
DEC_TQ = 8 * ATT_ROWS
CTX_SEQS_PER_STEP = 2
PAD = 16
LOCAL_CHUNK = 128
CONV_SHIFT_GROUPS = -(-(CONV_KERNEL + 1) // SUBLANES)
CONV_HALO = (CONV_SHIFT_GROUPS - 1) * SUBLANES

Q_SCALE = QK_HEAD_DIM ** -0.5 * math.log2(math.e)

VMEM_LIMIT = 56 * 1024 * 1024


def _cond_row(i):
    return jnp.where(i < N_CTX_TILES, 0, 1 + (i - N_CTX_TILES) // DEC_TILES_PER_SEQ)


def _ctx_tile(i):
    return (jnp.minimum(i, N_CTX_TILES - 1), 0)


def _dec_tile(i):
    return (jnp.maximum(i - N_CTX_TILES, 0), 0)


def _group_specs(width):
    return [pl.BlockSpec((TM, width), _ctx_tile), pl.BlockSpec((TM, width), _dec_tile)]


def _layer_norm(r, g, b):
    mu = jnp.mean(r, axis=-1, keepdims=True)
    d = r - mu
    var = jnp.mean(d * d, axis=-1, keepdims=True)
    return d * lax.rsqrt(var + LN_EPS) * g + b


def _dot(a, b):
    return jnp.dot(a, b, preferred_element_type=F32)


def _fixed_spec(tail, *lead):
    index = tuple(lead) + (0,) * len(tail)
    return pl.BlockSpec((None,) * len(lead) + tuple(tail), lambda *_: index,
                        pipeline_mode=pl.Buffered(1))


def _mod_spec(l):
    return pl.BlockSpec((None, 1, N_MOD, D_MODEL), lambda i: (l, _cond_row(i), 0, 0))


def _params(n_grid_dims):
    return pltpu.CompilerParams(dimension_semantics=("arbitrary",) * n_grid_dims,
                                vmem_limit_bytes=VMEM_LIMIT)


def _mod_kernel(cond_ref, w_ref, b_ref, o_ref):
    a = jax.nn.silu(cond_ref[...]).astype(BF16)
    o_ref[0] = _dot(a, w_ref[0].astype(BF16)) + b_ref[0]


def _modulation(cond, w_mod, b_mod):
    n_col = N_MOD * D_MODEL
    return pl.pallas_call(
        _mod_kernel,
        grid=(DEPTH, n_col // MOD_TN),
        in_specs=[
            pl.BlockSpec((COND_ROWS, D_MODEL), lambda l, j: (0, 0)),
            pl.BlockSpec((1, D_MODEL, MOD_TN), lambda l, j: (l, 0, j)),
            pl.BlockSpec((1, 1, MOD_TN), lambda l, j: (l, 0, j)),
        ],
        out_specs=pl.BlockSpec((1, COND_ROWS, MOD_TN), lambda l, j: (l, 0, j)),
        out_shape=jax.ShapeDtypeStruct((DEPTH, COND_ROWS, n_col), F32),
        compiler_params=_params(2),
        name="modulation",
    )(cond, w_mod, b_mod.reshape(DEPTH, 1, n_col))


def _ffn_kernel(*refs, sub, split_in, split_out):
    n_x = 2 if split_in else 1
    n_o = 2 if split_out else 1
    x_refs = refs[:n_x]
    mod_ref, win_ref, wout_ref, g_ref, b_ref = refs[n_x:n_x + 5]
    o_refs = refs[n_x + 5:n_x + 5 + n_o]
    act_ref = refs[n_x + 5 + n_o]
    is_ctx = pl.program_id(0) < N_CTX_TILES
    shift = mod_ref[0, 3 * sub:3 * sub + 1, :]
    scale = mod_ref[0, 3 * sub + 1:3 * sub + 2, :]
    gate = mod_ref[0, 3 * sub + 2:3 * sub + 3, :]

    def load_x(rows):
        if split_in:
            return jnp.where(is_ctx, x_refs[0][rows, :], x_refs[1][rows, :])
        return x_refs[0][rows, :]

    for s, rows in enumerate(_SUB_TILES):
        h = (load_x(rows) * (1.0 + scale) + shift).astype(BF16)
        for j in range(D_FF // FF_CHUNK):
            lo = j * FF_CHUNK
            g = _dot(h, win_ref[:, lo:lo + FF_CHUNK])
            u = _dot(h, win_ref[:, D_FF + lo:D_FF + lo + FF_CHUNK])
            act_ref[s, :, lo:lo + FF_CHUNK] = (jax.nn.silu(g) * u).astype(BF16)
    def finish(o_ref):
        for s, rows in enumerate(_SUB_TILES):
            y = _dot(act_ref[s], wout_ref[...])
            o_ref[rows, :] = _layer_norm(ALPHA * load_x(rows) + 0.5 * gate * y,
                                         g_ref[...], b_ref[...])

    if split_out:
        pl.when(is_ctx)(functools.partial(finish, o_refs[0]))
        pl.when(jnp.logical_not(is_ctx))(functools.partial(finish, o_refs[1]))
    else:
        finish(o_refs[0])


def _ffn(x, p, l, sub, split_out=False):
    split_in = isinstance(x, tuple)
    xs = x if split_in else (x,)
    row = pl.BlockSpec((TM, D_MODEL), lambda i: (i, 0))
    if split_out:
        out_specs = _group_specs(D_MODEL)
        out_shape = [jax.ShapeDtypeStruct((N_CTX_TOK, D_MODEL), F32),
                     jax.ShapeDtypeStruct((N_DEC_TOK, D_MODEL), F32)]
    else:
        out_specs = row
        out_shape = jax.ShapeDtypeStruct((N_TOK, D_MODEL), F32)
    return pl.pallas_call(
        functools.partial(_ffn_kernel, sub=sub, split_in=split_in, split_out=split_out),
        grid=(N_TILES,),
        in_specs=(_group_specs(D_MODEL) if split_in else [row]) + [
            _mod_spec(l),
            _fixed_spec((D_MODEL, 2 * D_FF), l, sub // 2),
            _fixed_spec((D_FF, D_MODEL), l, sub // 2),
            _fixed_spec((1, D_MODEL), l, sub),
            _fixed_spec((1, D_MODEL), l, sub),
        ],
        out_specs=out_specs,
        out_shape=out_shape,
        scratch_shapes=[pltpu.VMEM((N_SUB_TILES, TM // N_SUB_TILES, D_FF), BF16)],
        compiler_params=_params(1),
        name="ffn",
    )(*xs, p["mod"], p["w_ffn_in"], p["w_ffn_out"], p["ln_g"], p["ln_b"])


def _rope_tables():
    nf = QK_HEAD_DIM // 4
    t = np.arange(DEC_SEQ)
    lane = np.arange(V_HEAD_DIM)
    d = lane % QK_HEAD_DIM
    by_col = d >= QK_HEAD_DIM // 2
    e = d % (QK_HEAD_DIM // 2)
    inv = ROPE_BASE ** (-(e % nf).astype(np.float64) / nf)
    pos = np.where(by_col[None, :], (t % GRID_W)[:, None], (t // GRID_W)[:, None])
    ang = pos.astype(np.float64) * inv[None, :]
    first = (e < nf)[None, :]
    cos = np.cos(ang)
    sin_next = np.where(first, -np.sin(ang), 0.0)
    sin_prev = np.where(first, 0.0, np.sin(ang))
    ident = np.zeros((TM, V_HEAD_DIM))
    cos = np.concatenate([cos, ident + 1.0], axis=0)
    sin_next = np.concatenate([sin_next, ident], axis=0)
    sin_prev = np.concatenate([sin_prev, ident], axis=0)
    return (jnp.asarray(cos, F32), jnp.asarray(sin_next, F32), jnp.asarray(sin_prev, F32))


def _inproj_kernel(x_ref, mod_ref, w_ref, b_ref, cos_ref, sn_ref, sp_ref,
                   q_ref, k_ref, v_ref, u_ref, pu_ref, kf_ref, vf_ref):
    is_ctx = pl.program_id(0) < N_CTX_TILES
    x = x_ref[...]
    h = (x * (1.0 + mod_ref[0, 4:5, :]) + mod_ref[0, 3:4, :]).astype(BF16)
    cos = cos_ref[...]
    sn = sn_ref[...]
    sp = sp_ref[...]
    half = QK_HEAD_DIM // 4

    def proj(lo, width):
        return _dot(h, w_ref[:, lo:lo + width]) + b_ref[:, lo:lo + width]

    def rope(z):
        up = pltpu.roll(z, V_HEAD_DIM - half, axis=1)
        dn = pltpu.roll(z, half, axis=1)
        return z * cos + up * sn + dn * sp

    zq = proj(0, QK_COLS)
    zk = proj(QK_COLS, QK_COLS)
    for hd in range(ATT_HEADS):
        sl = slice(hd * V_HEAD_DIM, (hd + 1) * V_HEAD_DIM)
        q_ref[:, sl] = (rope(zq[:, sl]) * Q_SCALE).astype(BF16)
        k_ref[:, sl] = rope(zk[:, sl]).astype(BF16)
    zv = proj(2 * QK_COLS, ATT_WIDTH)
    v_ref[...] = zv.astype(BF16)
    c1 = 2 * QK_COLS + ATT_WIDTH
    zc = proj(c1, 2 * CONV_WIDTH)
    u_ref[...] = zc[:, :CONV_WIDTH] * jax.nn.sigmoid(zc[:, CONV_WIDTH:])
    pu_ref[...] = proj(c1 + 2 * CONV_WIDTH, POOL_WIDTH)

    @pl.when(is_ctx)
    def _():
        kf_ref[...] = zk
        vf_ref[...] = zv


def _inproj(x, p, l):
    cos, sn, sp = p["rope"]

    def tab_map(i):
        return (jnp.where(i < N_CTX_TILES, DEC_TILES_PER_SEQ,
                          (i - N_CTX_TILES) % DEC_TILES_PER_SEQ), 0)

    tab_spec = pl.BlockSpec((TM, V_HEAD_DIM), tab_map)
    row = lambda w: pl.BlockSpec((TM, w), lambda i: (i, 0))
    return pl.pallas_call(
        _inproj_kernel,
        grid=(N_TILES,),
        in_specs=[
            row(D_MODEL),
            _mod_spec(l),
            _fixed_spec((D_MODEL, PROJ_COLS), l),
            _fixed_spec((1, PROJ_COLS), l),
            tab_spec, tab_spec, tab_spec,
        ],
        out_specs=[
            row(QK_COLS), row(QK_COLS), row(ATT_WIDTH), row(CONV_WIDTH), row(POOL_WIDTH),
            pl.BlockSpec((TM, QK_COLS), _ctx_tile),
            pl.BlockSpec((TM, ATT_WIDTH), _ctx_tile),
        ],
        out_shape=[
            jax.ShapeDtypeStruct((N_TOK, QK_COLS), BF16),
            jax.ShapeDtypeStruct((N_TOK, QK_COLS), BF16),
            jax.ShapeDtypeStruct((N_TOK, ATT_WIDTH), BF16),
            jax.ShapeDtypeStruct((N_TOK, CONV_WIDTH), F32),
            jax.ShapeDtypeStruct((N_TOK, POOL_WIDTH), F32),
            jax.ShapeDtypeStruct((N_CTX_TOK, QK_COLS), F32),
            jax.ShapeDtypeStruct((N_CTX_TOK, ATT_WIDTH), F32),
        ],
        compiler_params=_params(1),
        name="inproj",
    )(x, p["mod"], p["w_proj"], p["b_in"], cos, sn, sp)


def _interleave(*stages):
    for i in range(max(len(st) for st in stages)):
        for st in stages:
            if i < len(st):
                st[i]()


def _attention_units(units, lam, lam_init, bufs, n_keys, group):
    n_chunks = n_keys // KEY_CHUNK
    n_tiles = n_keys // LANES
    n_blocks = 2 * ATT_ROWS // ROW_BLOCK
    nt = (((1,), (1,)), ((), ()))
    state = [dict() for _ in units]
    n_groups = len(units) // group

    def buf(n):
        return bufs[(n // group) % 2][n % group]

    def q_stage(n):
        s_ref = buf(n)[0]
        st = state[n]

        def first():
            q = units[n][0]()
            lane = lax.broadcasted_iota(jnp.int32, q.shape, 1)
            zero = jnp.zeros_like(q)
            st["qq"] = jnp.concatenate([jnp.where(lane < QK_HEAD_DIM, q, zero),
                                        jnp.where(lane >= QK_HEAD_DIM, q, zero)], axis=0)

        def chunk(c):
            cols = slice(c * KEY_CHUNK, (c + 1) * KEY_CHUNK)
            s_ref[:, cols] = lax.dot_general(st["qq"], units[n][1](c), nt,
                                             preferred_element_type=F32)
        return [first] + [functools.partial(chunk, c) for c in range(n_chunks)]

    def e_stage(n):
        s_ref, p_ref = buf(n)
        dens = state[n]["dens"] = [None] * n_blocks

        def block(r):
            rows = slice(r * ROW_BLOCK, (r + 1) * ROW_BLOCK)
            tile = lambda t: s_ref[rows, t * LANES:(t + 1) * LANES]
            m = functools.reduce(jnp.maximum, [tile(t) for t in range(n_tiles)])
            m = jnp.broadcast_to(jnp.max(m, axis=-1, keepdims=True), (ROW_BLOCK, LANES))
            d = None
            for t in range(n_tiles):
                e = jnp.exp2(tile(t) - m)
                d = e if d is None else d + e
                p_ref[rows, t * LANES:(t + 1) * LANES] = e.astype(BF16)
            dens[r] = jnp.sum(d, axis=-1, keepdims=True)
        return [functools.partial(block, r) for r in range(n_blocks)]

    def v_stage(n):
        p_ref = buf(n)[1]
        st = state[n]

        def first():
            dens = st["dens"]
            st["d1"] = jnp.concatenate(dens[:n_blocks // 2], axis=0)
            d2 = jnp.concatenate(dens[n_blocks // 2:], axis=0)
            st["c"] = (lam * st["d1"] / d2).astype(BF16)
            st["o"] = jnp.zeros((ATT_ROWS, V_HEAD_DIM), F32)

        def chunk(c):
            cols = slice(c * KEY_CHUNK, (c + 1) * KEY_CHUNK)
            w = p_ref[0:ATT_ROWS, cols] - st["c"] * p_ref[ATT_ROWS:2 * ATT_ROWS, cols]
            st["o"] = st["o"] + _dot(w, units[n][2](c))

        def last():
            o = st["o"] * (1.0 / st["d1"])
            y = o * lax.rsqrt(jnp.mean(o * o, axis=-1, keepdims=True) + LN_EPS)
            units[n][4]((y * units[n][3]()) * (1.0 - lam_init))
        return [first] + [functools.partial(chunk, c) for c in range(n_chunks)] + [last]

    for t in range(n_groups + 2):
        stages = []
        for stage, g in ((q_stage, t), (e_stage, t - 1), (v_stage, t - 2)):
            if 0 <= g < n_groups:
                stages += [stage(n) for n in range(g * group, (g + 1) * group)]
        _interleave(*stages)


def _attention_units_keys_on_rows(units, lam, lam_init, bufs, n_keys, group):
    n_chunks = n_keys // KEY_CHUNK
    width = 2 * ATT_ROWS
    nt = (((1,), (1,)), ((), ()))
    tn = (((0,), (0,)), ((), ()))
    state = [dict() for _ in units]
    n_groups = len(units) // group

    def fold(x, op):
        return functools.reduce(op, [x[r:r + SUBLANES, :] for r in range(0, KEY_CHUNK, SUBLANES)])

    def buf(n):
        return bufs[(n // group) % 2][n % group]

    def q_stage(n):
        s_ref = buf(n)[0]
        st = state[n]

        def first():
            q = units[n][0]()
            lane = lax.broadcasted_iota(jnp.int32, q.shape, 1)
            zero = jnp.zeros_like(q)
            st["qq"] = jnp.concatenate([jnp.where(lane < QK_HEAD_DIM, q, zero),
                                        jnp.where(lane >= QK_HEAD_DIM, q, zero)], axis=0)
            st["m"] = None

        def chunk(c):
            s = lax.dot_general(units[n][1](c), st["qq"], nt, preferred_element_type=F32)
            s_ref[c * KEY_CHUNK:(c + 1) * KEY_CHUNK, :] = s
            m = fold(s, jnp.maximum)
            st["m"] = m if st["m"] is None else jnp.maximum(st["m"], m)
        return [first] + [functools.partial(chunk, c) for c in range(n_chunks)]

    def e_stage(n):
        s_ref, p_ref = buf(n)
        st = state[n]

        def first():
            st["m_row"] = jnp.max(st["m"], axis=0, keepdims=True)
            st["d"] = None

        def chunk(c):
            rows = slice(c * KEY_CHUNK, (c + 1) * KEY_CHUNK)
            e = jnp.exp2(s_ref[rows, :] - st["m_row"])
            d = fold(e, jnp.add)
            st["d"] = d if st["d"] is None else st["d"] + d
            p_ref[rows, :] = e.astype(BF16)
        return [first] + [functools.partial(chunk, c) for c in range(n_chunks)]

    def v_stage(n):
        p_ref = buf(n)[1]
        st = state[n]

        def first():
            den = jnp.sum(st["d"], axis=0, keepdims=True)
            st["a"] = (1.0 / den[:, :ATT_ROWS]).astype(BF16)
            st["b"] = (lam / den[:, ATT_ROWS:]).astype(BF16)
            st["o"] = jnp.zeros((ATT_ROWS, V_HEAD_DIM), F32)

        def chunk(c):
            rows = slice(c * KEY_CHUNK, (c + 1) * KEY_CHUNK)
            w = p_ref[rows, 0:ATT_ROWS] * st["a"] - p_ref[rows, ATT_ROWS:width] * st["b"]
            st["o"] = st["o"] + lax.dot_general(w, units[n][2](c), tn,
                                                preferred_element_type=F32)

        def last():
            o = st["o"]
            y = o * lax.rsqrt(jnp.mean(o * o, axis=-1, keepdims=True) + LN_EPS)
            units[n][4]((y * units[n][3]()) * (1.0 - lam_init))
        return [first] + [functools.partial(chunk, c) for c in range(n_chunks)] + [last]

    for t in range(n_groups + 2):
        stages = []
        for stage, g in ((q_stage, t), (e_stage, t - 1), (v_stage, t - 2)):
            if 0 <= g < n_groups:
                stages += [stage(n) for n in range(g * group, (g + 1) * group)]
        _interleave(*stages)


def _lam_init(l):
    return 0.8 - 0.6 * math.exp(-0.3 * l)


def _lambda(lq_ref, lam_init):
    lq = lq_ref[...]
    return (jnp.exp(jnp.sum(lq[0:1] * lq[1:2], axis=-1, keepdims=True))
            - jnp.exp(jnp.sum(lq[2:3] * lq[3:4], axis=-1, keepdims=True)) + lam_init)


def _attn_ctx_kernel(lq_ref, g_ref, q_ref, k_ref, v_ref, o_ref, s_all, p_all, *, lam_init):
    def store(rows, cols, y):
        o_ref[rows, cols] = y.astype(o_ref.dtype)

    units = []
    for sq in range(CTX_SEQS_PER_STEP):
        rows = slice(sq * SEQ, (sq + 1) * SEQ)
        for hd in range(ATT_HEADS):
            cols = slice(hd * V_HEAD_DIM, (hd + 1) * V_HEAD_DIM)
            units.append((lambda rows=rows, cols=cols: q_ref[rows, cols],
                          lambda c, rows=rows, cols=cols: k_ref[rows, cols],
                          lambda c, rows=rows, cols=cols: v_ref[rows, cols],
                          lambda hd=hd: g_ref[hd],
                          functools.partial(store, rows, cols)))
    bufs = [[(s_all.at[par * ATT_HEADS + i], p_all.at[par * ATT_HEADS + i])
             for i in range(ATT_HEADS)] for par in range(2)]
    _attention_units_keys_on_rows(units, _lambda(lq_ref, lam_init), lam_init, bufs, SEQ,
                                  group=ATT_HEADS)


def _attention_ctx(q, k, v, p, l):
    rows = CTX_SEQS_PER_STEP * SEQ
    blk = pl.BlockSpec((rows, ATT_WIDTH), lambda b: (b, 0))
    s_buf = pltpu.VMEM((2 * ATT_HEADS, SEQ, 2 * ATT_ROWS), F32)
    p_buf = pltpu.VMEM((2 * ATT_HEADS, SEQ, 2 * ATT_ROWS), BF16)
    return pl.pallas_call(
        functools.partial(_attn_ctx_kernel, lam_init=_lam_init(l)),
        grid=(BATCH // CTX_SEQS_PER_STEP,),
        in_specs=[
            _fixed_spec((4, QK_HEAD_DIM), l),
            _fixed_spec((ATT_HEADS, 1, V_HEAD_DIM), l),
            blk, blk, blk,
        ],
        out_specs=blk,
        out_shape=jax.ShapeDtypeStruct((N_CTX_TOK, ATT_WIDTH), BF16),
        scratch_shapes=[s_buf, p_buf],
        compiler_params=_params(1),
        name="attention_ctx",
    )(p["lambda_qk"], p["subln_g"], q, k, v)


def _attn_dec_kernel(lq_ref, g_ref, q_ref, pk_ref, pv_ref, k_ref, v_ref, o_ref,
                     kcat, vcat, s0, s1, p0, p1, *, lam_init):
    def gather_keys_values():
        kcat[0:PAST_LEN, :] = pk_ref[...].astype(BF16)
        kcat[PAST_LEN:PAST_LEN + DEC_SEQ, :] = k_ref[...]
        vcat[0:PAST_LEN, :] = pv_ref[...].astype(BF16)
        vcat[PAST_LEN:PAST_LEN + DEC_SEQ, :] = v_ref[...]

    if DEC_TQ == DEC_SEQ:
        gather_keys_values()
    else:
        pl.when(pl.program_id(2) == 0)(gather_keys_values)

    def store(rows, y):
        o_ref[rows, :] = y.astype(o_ref.dtype)

    def chunk_of(ref):
        return lambda c: ref[c * KEY_CHUNK:(c + 1) * KEY_CHUNK, :]

    units = []
    for r0 in range(0, DEC_TQ, ATT_ROWS):
        rows = slice(r0, r0 + ATT_ROWS)
        units.append((lambda rows=rows: q_ref[rows, :], chunk_of(kcat), chunk_of(vcat),
                      lambda: g_ref[0], functools.partial(store, rows)))
    _attention_units(units, _lambda(lq_ref, lam_init), lam_init, [[(s0, p0)], [(s1, p1)]],
                     PAST_LEN + DEC_SEQ, group=1)


def _attention_dec(q, k, v, p, l):
    nq = DEC_SEQ // DEC_TQ
    q0 = N_CTX_TOK // DEC_TQ
    s0 = N_CTX_TOK // DEC_SEQ
    n_keys = PAST_LEN + DEC_SEQ
    past = pl.BlockSpec((None, None, PAST_LEN, V_HEAD_DIM), lambda b, h, j: (b, l, 0, h))
    own = pl.BlockSpec((DEC_SEQ, V_HEAD_DIM), lambda b, h, j: (s0 + b, h))
    kv_buf = pltpu.VMEM((n_keys, V_HEAD_DIM), BF16)
    s_buf = pltpu.VMEM((2 * ATT_ROWS, n_keys), F32)
    p_buf = pltpu.VMEM((2 * ATT_ROWS, n_keys), BF16)
    return pl.pallas_call(
        functools.partial(_attn_dec_kernel, lam_init=_lam_init(l)),
        grid=(DEC_BATCH, ATT_HEADS, nq),
        in_specs=[
            _fixed_spec((4, QK_HEAD_DIM), l),
            pl.BlockSpec((None, 1, 1, V_HEAD_DIM), lambda b, h, j: (l, h, 0, 0)),
            pl.BlockSpec((DEC_TQ, V_HEAD_DIM), lambda b, h, j: (q0 + b * nq + j, h)),
            past, past, own, own,
        ],
        out_specs=pl.BlockSpec((DEC_TQ, V_HEAD_DIM), lambda b, h, j: (b * nq + j, h)),
        out_shape=jax.ShapeDtypeStruct((N_DEC_TOK, ATT_WIDTH), BF16),
        scratch_shapes=[kv_buf, kv_buf, s_buf, s_buf, p_buf, p_buf],
        compiler_params=_params(3),
        name="attention_dec",
    )(p["lambda_qk"], p["subln_g"], q, p["past_k"], p["past_v"], k, v)


def _local_kernel(u_ref, pu_ref, cw_ref, cb_ref, g_ref, b_ref, cu_ref, pd_ref,
                  upad, ppad, s2, s4, s8, shifted, *, seq):
    zpad = jnp.zeros((PAD, CONV_WIDTH), F32)
    for buf in (upad, ppad, s2, s4, s8):
        buf[0:PAD, :] = zpad
        buf[PAD + seq:PAD + seq + PAD, :] = zpad
    upad[PAD:PAD + seq, :] = u_ref[...]
    ppad[PAD:PAD + seq, :] = pu_ref[...]

    n_chunks = seq // LOCAL_CHUNK
    r = LOCAL_CHUNK
    centre = CONV_KERNEL // 2
    n_shift_groups = CONV_SHIFT_GROUPS
    halo = CONV_HALO

    for c in range(n_chunks):
        win0 = c * r
        for shift in range(1, SUBLANES):
            shifted[shift] = upad[win0 + shift:win0 + shift + r + halo, :]
        acc = jnp.zeros((r, CONV_WIDTH), F32)
        for shift in range(SUBLANES):
            for a in range(n_shift_groups):
                t = SUBLANES * a + shift - (PAD - centre)
                if 0 <= t < CONV_KERNEL:
                    lo = SUBLANES * a
                    window = (shifted[shift, lo:lo + r, :] if shift
                              else upad[win0 + lo:win0 + lo + r, :])
                    acc = acc + window * cw_ref[t:t + 1, :]
        y = _layer_norm(acc + cb_ref[...], g_ref[...], b_ref[...])
        cu_ref[c * r:(c + 1) * r, :] = jax.nn.silu(y).astype(cu_ref.dtype)

    ext = seq + PAD
    s2[8:8 + ext, :] = ppad[7:7 + ext, :] + ppad[8:8 + ext, :]
    s4[8:8 + ext, :] = s2[7:7 + ext, :] + s2[9:9 + ext, :]
    s8[8:8 + ext, :] = s4[6:6 + ext, :] + s4[10:10 + ext, :]
    lane = lax.broadcasted_iota(jnp.int32, (r, POOL_WIDTH), 1)
    grp = lane // POOL_GROUP_DIM
    half_win = jnp.where(grp == 0, 1, jnp.where(grp == 1, 2, jnp.where(grp == 2, 4, 8)))
    for c in range(n_chunks):
        base = PAD + c * r
        sl = slice(base, base + r)
        s16 = s8[base - 4:base - 4 + r, :] + s8[base + 4:base + 4 + r, :]
        tot = jnp.where(grp == 0, s2[sl, :],
                        jnp.where(grp == 1, s4[sl, :], jnp.where(grp == 2, s8[sl, :], s16)))
        t = lax.broadcasted_iota(jnp.int32, (r, POOL_WIDTH), 0) + c * r
        cnt = jnp.clip(t + half_win, 0, seq) - jnp.clip(t - half_win, 0, seq)
        pd_ref[c * r:(c + 1) * r, :] = (tot / cnt.astype(F32) - ppad[sl, :]).astype(pd_ref.dtype)


def _local(u, pu, p, l, *, n_seq, seq, row0):
    s0 = row0 // seq
    blk = pl.BlockSpec((seq, CONV_WIDTH), lambda b: (s0 + b, 0))
    out = pl.BlockSpec((seq, CONV_WIDTH), lambda b: (b, 0))
    vec = _fixed_spec((1, CONV_WIDTH), l)
    pad_buf = pltpu.VMEM((seq + 2 * PAD, CONV_WIDTH), F32)
    return pl.pallas_call(
        functools.partial(_local_kernel, seq=seq),
        grid=(n_seq,),
        in_specs=[blk, blk, _fixed_spec((CONV_KERNEL, CONV_WIDTH), l), vec, vec, vec],
        out_specs=[out, out],
        out_shape=[jax.ShapeDtypeStruct((n_seq * seq, CONV_WIDTH), BF16),
                   jax.ShapeDtypeStruct((n_seq * seq, POOL_WIDTH), BF16)],
        scratch_shapes=[pad_buf] * 5 + [
            pltpu.VMEM((SUBLANES, LOCAL_CHUNK + CONV_HALO, CONV_WIDTH), F32)],
        compiler_params=_params(1),
        name="local_mixers",
    )(u, pu, p["conv_dw_w"], p["conv_dw_b"], p["conv_ln_g"], p["conv_ln_b"])


def _merge_kernel(x_ref, mod_ref, oc_ref, od_ref, cuc_ref, cud_ref, pdc_ref, pdd_ref,
                  wg_ref, bg_ref, wa_ref, wc_ref, wgrp_ref, ps_ref, wp_ref, wo_ref, bo_ref,
                  g_ref, b_ref, out_ref, merged_ref):
    is_ctx = pl.program_id(0) < N_CTX_TILES
    cw = 256
    for rows in _SUB_TILES:
        h = (x_ref[rows, :] * (1.0 + mod_ref[0, 4:5, :]) + mod_ref[0, 3:4, :]).astype(BF16)
        att = jnp.where(is_ctx, oc_ref[rows, :], od_ref[rows, :])
        cu = jnp.where(is_ctx, cuc_ref[rows, :], cud_ref[rows, :])
        pd = jnp.where(is_ctx, pdc_ref[rows, :], pdd_ref[rows, :])
        pooled = (_dot(pd, wgrp_ref[...]) * ps_ref[...]).astype(BF16)
        for j in range(D_MODEL // cw):
            sl = slice(j * cw, (j + 1) * cw)
            branches = (_dot(att, wa_ref[:, sl]), _dot(cu, wc_ref[:, sl]),
                        _dot(pooled, wp_ref[:, sl]))
            merged = None
            for n, br in enumerate(branches):
                gs = slice(n * D_MODEL + j * cw, n * D_MODEL + (j + 1) * cw)
                gate = jax.nn.sigmoid(_dot(h, wg_ref[:, gs]) + bg_ref[:, gs])
                merged = gate * br if merged is None else merged + gate * br
            merged_ref[rows, sl] = merged.astype(BF16)
    for rows in _SUB_TILES:
        mix = _dot(merged_ref[rows, :], wo_ref[...]) + bo_ref[...]
        out_ref[rows, :] = _layer_norm(ALPHA * x_ref[rows, :] + mod_ref[0, 5:6, :] * mix,
                                       g_ref[...], b_ref[...])


def _merge(x, o, cu, pd, p, l):
    row = lambda w: pl.BlockSpec((TM, w), lambda i: (i, 0))
    return pl.pallas_call(
        _merge_kernel,
        grid=(N_TILES,),
        in_specs=[
            row(D_MODEL),
            _mod_spec(l),
            *_group_specs(ATT_WIDTH), *_group_specs(CONV_WIDTH), *_group_specs(POOL_WIDTH),
            _fixed_spec((D_MODEL, GATE_COLS), l), _fixed_spec((1, GATE_COLS), l),
            _fixed_spec((ATT_WIDTH, D_MODEL), l), _fixed_spec((CONV_WIDTH, D_MODEL), l),
            _fixed_spec((POOL_WIDTH, POOL_WIDTH), l), _fixed_spec((1, POOL_WIDTH), l),
            _fixed_spec((POOL_WIDTH, D_MODEL), l),
            _fixed_spec((D_MODEL, D_MODEL), l), _fixed_spec((1, D_MODEL), l),
            _fixed_spec((1, D_MODEL), l, 1), _fixed_spec((1, D_MODEL), l, 1),
        ],
        out_specs=row(D_MODEL),
        out_shape=jax.ShapeDtypeStruct((N_TOK, D_MODEL), F32),
        scratch_shapes=[pltpu.VMEM((TM, D_MODEL), BF16)],
        compiler_params=_params(1),
        name="merge",
    )(x, p["mod"], *o, *cu, *pd, p["w_gate"], p["b_gate"], p["w_att_o"], p["w_conv_o"],
      p["w_pool_g"], p["pool_scale"], p["w_pool_o"], p["w_out"], p["b_out"], p["ln_g"], p["ln_b"])


def _block_diag(w_grp):
    eye = jnp.eye(POOL_GROUPS, dtype=w_grp.dtype)
    return jnp.einsum("lgcd,gh->lgchd", w_grp, eye).reshape(-1, POOL_WIDTH, POOL_WIDTH)


def kernel(x_prompt, x_sample, cache_k, cache_v, c, c_ctx, w_mod, b_mod, w_ffn_in, w_ffn_out, ln_g, ln_b, w_in, b_in, lambda_qk, subln_g, w_att_o, conv_dw_w, conv_dw_b, conv_ln_g, conv_ln_b, w_conv_o, w_pool_g, pool_scale, w_pool_o, w_out, b_out):
    x = (x_prompt.reshape(N_CTX_TOK, D_MODEL), x_sample.reshape(N_DEC_TOK, D_MODEL))
    cond = jnp.zeros((COND_ROWS, D_MODEL), F32).at[0].set(c_ctx).at[1:1 + DEC_BATCH].set(c)
    rows = lambda a: a.reshape(a.shape[:-1] + (1, a.shape[-1]))
    p = dict(
        mod=_modulation(cond, w_mod, b_mod).reshape(DEPTH, COND_ROWS, N_MOD, D_MODEL),
        rope=_rope_tables(),
        past_k=cache_k.reshape(DEC_BATCH, DEPTH, PAST_LEN, QK_COLS),
        past_v=cache_v.reshape(DEC_BATCH, DEPTH, PAST_LEN, ATT_WIDTH),
        w_ffn_in=w_ffn_in.astype(BF16), w_ffn_out=w_ffn_out.astype(BF16),
        ln_g=rows(ln_g), ln_b=rows(ln_b),
        w_proj=w_in[:, :, :PROJ_COLS].astype(BF16), b_in=rows(b_in),
        w_gate=w_in[:, :, PROJ_COLS:].astype(BF16), b_gate=rows(b_in[:, PROJ_COLS:]),
        lambda_qk=lambda_qk, subln_g=rows(subln_g),
        conv_dw_w=conv_dw_w, conv_dw_b=rows(conv_dw_b),
        conv_ln_g=rows(conv_ln_g), conv_ln_b=rows(conv_ln_b),
        w_att_o=w_att_o.astype(BF16), w_conv_o=w_conv_o.astype(BF16),
        w_pool_g=_block_diag(w_pool_g).astype(BF16), pool_scale=rows(pool_scale),
        w_pool_o=w_pool_o.astype(BF16), w_out=w_out.astype(BF16), b_out=rows(b_out),
    )

    new_k, new_v = [], []
    for l in range(DEPTH):
        x = _ffn(x, p, l, sub=0)
        q, k, v, u, pu, kf, vf = _inproj(x, p, l)
        new_k.append(kf.reshape(BATCH, SEQ, ATT_HEADS, 2, QK_HEAD_DIM))
        new_v.append(vf.reshape(BATCH, SEQ, ATT_HEADS, V_HEAD_DIM))
        o = (_attention_ctx(q, k, v, p, l), _attention_dec(q, k, v, p, l))
        cu_ctx, pd_ctx = _local(u, pu, p, l, n_seq=BATCH, seq=SEQ, row0=0)
        cu_dec, pd_dec = _local(u, pu, p, l, n_seq=DEC_BATCH, seq=DEC_SEQ, row0=N_CTX_TOK)
        x = _merge(x, o, (cu_ctx, cu_dec), (pd_ctx, pd_dec), p, l)
        x = _ffn(x, p, l, sub=2, split_out=(l == DEPTH - 1))

    y_prompt = x[0].reshape(BATCH, SEQ, D_MODEL)
    y_sample = x[1].reshape(DEC_BATCH, DEC_SEQ, D_MODEL)
    return (y_prompt, y_sample, jnp.stack(new_k, axis=1), jnp.stack(new_v, axis=1))
```
